```python
import math
import jax, jax.numpy as jnp
from jax import lax
import numpy as np

D_MODEL = 1024
BATCH = 8
SEQ = 2048
DEPTH = 4

N_MIXERS = 2
N_GDN_LAYERS = (DEPTH + N_MIXERS - 1) // N_MIXERS
N_MLA_LAYERS = DEPTH // N_MIXERS

GDN_HEADS = 8
GDN_HEAD_DIM = 128
GDN_KEY_DIM = GDN_HEADS * GDN_HEAD_DIM
GDN_VALUE_DIM = GDN_HEADS * GDN_HEAD_DIM
GDN_CONV = 4
GDN_CHUNK = 64
GDN_IN = 2 * GDN_KEY_DIM + 2 * GDN_VALUE_DIM + 2 * GDN_HEADS

MLA_HEADS = 8
MLA_NOPE = 128
MLA_ROPE = 64
MLA_V = 128
MLA_Q_RANK = 384
MLA_KV_RANK = 256
MLA_IN = MLA_Q_RANK + MLA_KV_RANK + MLA_ROPE
ROPE_THETA = 10000.0
Q_BLOCK = 128

D_FF = ((8 * D_MODEL + 3 * 256 - 1) // (3 * 256)) * 256
N_MOD = 6
EPS = 1e-6

kernel_name = "hybrid_gdn_mla_adaln_trunk"


def rmsnorm(x, g):
    xf = x.astype(jnp.float32)
    y = xf * lax.rsqrt(jnp.mean(xf * xf, axis=-1, keepdims=True) + EPS)
    return (y * g.astype(jnp.float32)).astype(x.dtype)


def l2norm(x):
    return x * lax.rsqrt(jnp.sum(x * x, axis=-1, keepdims=True) + EPS)


def causal_depthwise_conv(x, w):
    K = w.shape[-1]
    kern = jnp.transpose(w)[:, None, :].astype(x.dtype)
    return lax.conv_general_dilated(
        x, kern, window_strides=(1,), padding=[(K - 1, 0)],
        dimension_numbers=("NWC", "WIO", "NWC"), feature_group_count=x.shape[-1])


def chunk_gated_delta_rule(q, k, v, g, beta):
    B, H, T, Dk = q.shape
    Dv = v.shape[-1]
    C = GDN_CHUNK
    N = T // C
    q = q.reshape(B, H, N, C, Dk)
    k = k.reshape(B, H, N, C, Dk)
    v = v.reshape(B, H, N, C, Dv)
    g_cum = jnp.cumsum(g.reshape(B, H, N, C), axis=-1)
    beta = beta.reshape(B, H, N, C)

    tril = jnp.tril(jnp.ones((C, C), dtype=bool))
    strict = jnp.tril(jnp.ones((C, C), dtype=bool), k=-1)
    diff = g_cum[..., :, None] - g_cum[..., None, :]
    decay = jnp.exp(jnp.where(tril, diff, -jnp.inf))

    kb = k * beta[..., None]
    L = jnp.where(strict, jnp.einsum("bhnid,bhnjd->bhnij", kb, k) * decay, 0.0)
    A = L + jnp.eye(C, dtype=L.dtype)
    u = lax.linalg.triangular_solve(A, v * beta[..., None], left_side=True, lower=True,
                                    unit_diagonal=True)
    w = lax.linalg.triangular_solve(A, kb * jnp.exp(g_cum)[..., None], left_side=True,
                                    lower=True, unit_diagonal=True)
    attn = jnp.einsum("bhnid,bhnjd->bhnij", q, k) * decay
    q_dec = q * jnp.exp(g_cum)[..., None]
    k_dec = k * jnp.exp(g_cum[..., -1:] - g_cum)[..., None]
    g_last = jnp.exp(g_cum[..., -1])

    def step(S, xs):
        q_i, k_i, w_i, u_i, a_i, gl_i = xs
        v_new = u_i - jnp.einsum("bhck,bhkv->bhcv", w_i, S)
        o_i = jnp.einsum("bhck,bhkv->bhcv", q_i, S) + jnp.einsum("bhij,bhjv->bhiv", a_i, v_new)
        S = S * gl_i[..., None, None] + jnp.einsum("bhck,bhcv->bhkv", k_i, v_new)
        return S, o_i

    xs = tuple(jnp.moveaxis(t, 2, 0) for t in (q_dec, k_dec, w, u, attn, g_last))
    S0 = jnp.zeros((B, H, Dk, Dv), dtype=jnp.float32)
    _, o = lax.scan(step, S0, xs)
    return jnp.moveaxis(o, 0, 2).reshape(B, H, T, Dv)


def gdn_mixer(h, w_in, conv_w, a_log, dt_bias, norm_g, w_out):
    B, T, _ = h.shape
    proj = h @ w_in
    qkv = proj[..., :2 * GDN_KEY_DIM + GDN_VALUE_DIM]
    o0 = 2 * GDN_KEY_DIM + GDN_VALUE_DIM
    gate = proj[..., o0:o0 + GDN_VALUE_DIM]
    a_in = proj[..., o0 + GDN_VALUE_DIM:o0 + GDN_VALUE_DIM + GDN_HEADS]
    b_in = proj[..., o0 + GDN_VALUE_DIM + GDN_HEADS:]

    qkv = jax.nn.silu(causal_depthwise_conv(qkv, conv_w)).astype(jnp.float32)
    q = qkv[..., :GDN_KEY_DIM].reshape(B, T, GDN_HEADS, GDN_HEAD_DIM)
    k = qkv[..., GDN_KEY_DIM:2 * GDN_KEY_DIM].reshape(B, T, GDN_HEADS, GDN_HEAD_DIM)
    v = qkv[..., 2 * GDN_KEY_DIM:].reshape(B, T, GDN_HEADS, GDN_HEAD_DIM)
    q = l2norm(q) * (GDN_HEAD_DIM ** -0.5)
    k = l2norm(k)
    beta = jax.nn.sigmoid(b_in.astype(jnp.float32))
    g = -jnp.exp(a_log.astype(jnp.float32)) * jax.nn.softplus(
        a_in.astype(jnp.float32) + dt_bias.astype(jnp.float32))

    to_bhtd = lambda t: jnp.transpose(t, (0, 2, 1, 3))
    o = chunk_gated_delta_rule(to_bhtd(q), to_bhtd(k), to_bhtd(v),
                               jnp.transpose(g, (0, 2, 1)), jnp.transpose(beta, (0, 2, 1)))
    o = jnp.transpose(o, (0, 2, 1, 3)).astype(h.dtype)
    o = rmsnorm(o, norm_g) * jax.nn.silu(gate.reshape(B, T, GDN_HEADS, GDN_HEAD_DIM))
    return o.reshape(B, T, GDN_VALUE_DIM) @ w_out


def apply_rope(x, cos, sin):
    half = x.shape[-1] // 2
    x1, x2 = x[..., :half], x[..., half:]
    return jnp.concatenate([x1 * cos - x2 * sin, x2 * cos + x1 * sin], axis=-1)


def causal_mla_attention(q_nope, q_rope, k_nope, k_rope, v):
    B, T, H, _ = q_nope.shape
    nb = T // Q_BLOCK
    scale = (MLA_NOPE + MLA_ROPE) ** -0.5
    qn = jnp.moveaxis(q_nope.reshape(B, nb, Q_BLOCK, H, MLA_NOPE), 1, 0)
    qr = jnp.moveaxis(q_rope.reshape(B, nb, Q_BLOCK, H, MLA_ROPE), 1, 0)
    kpos = jnp.arange(T)

    def block(args):
        qn_b, qr_b, start = args
        s = (jnp.einsum("bqhd,bkhd->bhqk", qn_b, k_nope)
             + jnp.einsum("bqhd,bkd->bhqk", qr_b, k_rope)).astype(jnp.float32) * scale
        qpos = start + jnp.arange(Q_BLOCK)
        s = jnp.where(kpos[None, :] <= qpos[:, None], s, -jnp.inf)
        p = jax.nn.softmax(s, axis=-1).astype(v.dtype)
        return jnp.einsum("bhqk,bkhd->bqhd", p, v)

    o = lax.map(block, (qn, qr, jnp.arange(nb) * Q_BLOCK))
    return jnp.moveaxis(o, 0, 1).reshape(B, T, H, MLA_V)


def mla_mixer(h, cos, sin, w_in, q_norm_g, kv_norm_g, w_uq, w_ukv, w_out):
    B, T, _ = h.shape
    proj = h @ w_in
    c_q = proj[..., :MLA_Q_RANK]
    c_kv = proj[..., MLA_Q_RANK:MLA_Q_RANK + MLA_KV_RANK]
    k_rope = proj[..., MLA_Q_RANK + MLA_KV_RANK:]
    q = (rmsnorm(c_q, q_norm_g) @ w_uq).reshape(B, T, MLA_HEADS, MLA_NOPE + MLA_ROPE)
    kv = (rmsnorm(c_kv, kv_norm_g) @ w_ukv).reshape(B, T, MLA_HEADS, MLA_NOPE + MLA_V)
    q_nope, q_rope = q[..., :MLA_NOPE], q[..., MLA_NOPE:]
    k_nope, v = kv[..., :MLA_NOPE], kv[..., MLA_NOPE:]
    q_rope = apply_rope(q_rope, cos[:, :, None, :], sin[:, :, None, :])
    k_rope = apply_rope(k_rope, cos, sin)
    o = causal_mla_attention(q_nope, q_rope, k_nope, k_rope, v)
    return o.reshape(B, T, MLA_HEADS * MLA_V) @ w_out


def swiglu(h, w_gate, w_up, w_down):
    return (jax.nn.silu(h @ w_gate) * (h @ w_up)) @ w_down


def setup_inputs(seed: int = 0) -> dict:
    key = jax.random.key(seed)
    ks = jax.random.split(key, 24)
    f32 = jnp.float32
    nrm = lambda k, shape, s: jax.random.normal(k, shape, f32) * s
    x = jax.random.normal(ks[0], (BATCH, SEQ, D_MODEL), f32)
    c = jax.random.normal(ks[1], (BATCH, D_MODEL), f32)
    positions = (jnp.arange(SEQ, dtype=jnp.int32)[None, :]
                 + jax.random.randint(ks[2], (BATCH, 1), 0, 1024, dtype=jnp.int32))
    ada_w = nrm(ks[3], (DEPTH, D_MODEL, N_MOD * D_MODEL), 0.5 * D_MODEL ** -0.5)
    ada_b = nrm(ks[4], (DEPTH, N_MOD * D_MODEL), 0.02)
    norm_mix_g = 1.0 + nrm(ks[5], (DEPTH, D_MODEL), 0.05)
    norm_ffn_g = 1.0 + nrm(ks[6], (DEPTH, D_MODEL), 0.05)

    gdn_w_in = nrm(ks[7], (N_GDN_LAYERS, D_MODEL, GDN_IN), D_MODEL ** -0.5)
    gdn_conv_w = nrm(ks[8], (N_GDN_LAYERS, 2 * GDN_KEY_DIM + GDN_VALUE_DIM, GDN_CONV),
                     GDN_CONV ** -0.5)
    gdn_a_log = jnp.log(jax.random.uniform(ks[9], (N_GDN_LAYERS, GDN_HEADS), f32, 1.0, 16.0))
    dt = jnp.exp(jax.random.uniform(ks[10], (N_GDN_LAYERS, GDN_HEADS), f32,
                                    math.log(1e-3), math.log(1e-1)))
    gdn_dt_bias = dt + jnp.log(-jnp.expm1(-dt))
    gdn_norm_g = 1.0 + nrm(ks[11], (N_GDN_LAYERS, GDN_HEAD_DIM), 0.05)
    gdn_w_out = nrm(ks[12], (N_GDN_LAYERS, GDN_VALUE_DIM, D_MODEL), GDN_VALUE_DIM ** -0.5)

    mla_w_in = nrm(ks[13], (N_MLA_LAYERS, D_MODEL, MLA_IN), D_MODEL ** -0.5)
    mla_q_norm_g = 1.0 + nrm(ks[14], (N_MLA_LAYERS, MLA_Q_RANK), 0.05)
    mla_kv_norm_g = 1.0 + nrm(ks[15], (N_MLA_LAYERS, MLA_KV_RANK), 0.05)
    mla_w_uq = nrm(ks[16], (N_MLA_LAYERS, MLA_Q_RANK, MLA_HEADS * (MLA_NOPE + MLA_ROPE)),
                   MLA_Q_RANK ** -0.5)
    mla_w_ukv = nrm(ks[17], (N_MLA_LAYERS, MLA_KV_RANK, MLA_HEADS * (MLA_NOPE + MLA_V)),
                    MLA_KV_RANK ** -0.5)
    mla_w_out = nrm(ks[18], (N_MLA_LAYERS, MLA_HEADS * MLA_V, D_MODEL),
                    (MLA_HEADS * MLA_V) ** -0.5)

    ffn_w_gate = nrm(ks[19], (DEPTH, D_MODEL, D_FF), D_MODEL ** -0.5)
    ffn_w_up = nrm(ks[20], (DEPTH, D_MODEL, D_FF), D_MODEL ** -0.5)
    ffn_w_down = nrm(ks[21], (DEPTH, D_FF, D_MODEL), D_FF ** -0.5)
    final_norm_g = 1.0 + nrm(ks[22], (D_MODEL,), 0.05)
    return {
        "x": x, "c": c, "positions": positions,
        "ada_w": ada_w, "ada_b": ada_b, "norm_mix_g": norm_mix_g, "norm_ffn_g": norm_ffn_g,
        "gdn_w_in": gdn_w_in, "gdn_conv_w": gdn_conv_w, "gdn_a_log": gdn_a_log,
        "gdn_dt_bias": gdn_dt_bias, "gdn_norm_g": gdn_norm_g, "gdn_w_out": gdn_w_out,
        "mla_w_in": mla_w_in, "mla_q_norm_g": mla_q_norm_g, "mla_kv_norm_g": mla_kv_norm_g,
        "mla_w_uq": mla_w_uq, "mla_w_ukv": mla_w_ukv, "mla_w_out": mla_w_out,
        "ffn_w_gate": ffn_w_gate, "ffn_w_up": ffn_w_up, "ffn_w_down": ffn_w_down,
        "final_norm_g": final_norm_g,
    }


def reference(x, c, positions, ada_w, ada_b, norm_mix_g, norm_ffn_g,
              gdn_w_in, gdn_conv_w, gdn_a_log, gdn_dt_bias, gdn_norm_g, gdn_w_out,
              mla_w_in, mla_q_norm_g, mla_kv_norm_g, mla_w_uq, mla_w_ukv, mla_w_out,
              ffn_w_gate, ffn_w_up, ffn_w_down, final_norm_g):
    inv_freq = ROPE_THETA ** (-jnp.arange(0, MLA_ROPE, 2, dtype=jnp.float32) / MLA_ROPE)
    ang = positions.astype(jnp.float32)[..., None] * inv_freq
    cos = jnp.cos(ang).astype(x.dtype)
    sin = jnp.sin(ang).astype(x.dtype)
    c_act = jax.nn.silu(c)

    for layer in range(DEPTH):
        mod = c_act @ ada_w[layer] + ada_b[layer]
        shift_m, scale_m, gate_m, shift_f, scale_f, gate_f = [
            m[:, None, :] for m in jnp.split(mod, N_MOD, axis=-1)]

        h = rmsnorm(x, norm_mix_g[layer]) * (1.0 + scale_m) + shift_m
        j = layer // N_MIXERS
        if layer % N_MIXERS == 0:
            y = gdn_mixer(h, gdn_w_in[j], gdn_conv_w[j], gdn_a_log[j], gdn_dt_bias[j],
                          gdn_norm_g[j], gdn_w_out[j])
        else:
            y = mla_mixer(h, cos, sin, mla_w_in[j], mla_q_norm_g[j], mla_kv_norm_g[j],
                          mla_w_uq[j], mla_w_ukv[j], mla_w_out[j])
        x = x + gate_m * y

        h = rmsnorm(x, norm_ffn_g[layer]) * (1.0 + scale_f) + shift_f
        x = x + gate_f * swiglu(h, ffn_w_gate[layer], ffn_w_up[layer], ffn_w_down[layer])

    return rmsnorm(x, final_norm_g)
```

```python
import functools

import jax
import jax.numpy as jnp
from jax import lax
from jax.experimental import pallas as pl
from jax.experimental.pallas import tpu as pltpu

F32 = jnp.float32
BF16 = jnp.bfloat16

D_MODEL = 1024
N_MOD = 6
EPS = 1e-6

GDN_HEADS = 8
GDN_HEAD_DIM = 128
GDN_CONV = 4
GDN_CHUNK = 64
GDN_GROUP = 4
CONV_SLAB = 256
CONV_HALO = 16

MLA_HEADS = 8
MLA_NOPE = 128
MLA_ROPE = 64
MLA_V = 128
MLA_Q_RANK = 384
MLA_KV_RANK = 256
MLA_QK_PAD = 256
ROPE_THETA = 10000.0

ROW_TILE = 512
ATTN_BLOCK = 512
VMEM_LIMIT = 56 * 1024 * 1024


def _cparams(sem, vmem=VMEM_LIMIT):
    return pltpu.CompilerParams(dimension_semantics=sem, vmem_limit_bytes=vmem)


def _sigmoid(x):
    return 1.0 / (1.0 + jnp.exp(-x))


def _silu(x):
    return x * _sigmoid(x)


def _softplus(x):
    return jnp.maximum(x, 0.0) + jnp.log(1.0 + jnp.exp(-jnp.abs(x)))


def _dot(a, b):
    return jnp.dot(a, b, preferred_element_type=F32)


def _dot_nt(a, b):
    return lax.dot_general(a, b, (((1,), (1,)), ((), ())), preferred_element_type=F32)


def _split(a):
    hi = a.astype(BF16)
    lo = (a - hi.astype(F32)).astype(BF16)
    return hi, lo


def _dot3(a, b):
    ah, al = _split(a)
    bh, bl = _split(b)
    return _dot(ah, bh) + (_dot(ah, bl) + _dot(al, bh))


def _norm_mod(x, g, scale, shift):
    r = lax.rsqrt(jnp.mean(x * x, axis=-1, keepdims=True) + EPS)
    return (x * r) * g * (1.0 + scale) + shift


def _adaln_kernel(c_ref, w_ref, b_ref, o_ref):
    c_act = _silu(c_ref[...]).astype(BF16)
    o_ref[0] = _dot(c_act, w_ref[0].astype(BF16)) + b_ref[0]


def _adaln(c, ada_w, ada_b):
    depth, d, n = ada_w.shape
    bsz = c.shape[0]
    tn = 1536
    return pl.pallas_call(
        _adaln_kernel,
        grid=(depth, n // tn),
        in_specs=[
            pl.BlockSpec((bsz, d), lambda l, j: (0, 0)),
            pl.BlockSpec((1, d, tn), lambda l, j: (l, 0, j)),
            pl.BlockSpec((1, 1, tn), lambda l, j: (l, 0, j)),
        ],
        out_specs=pl.BlockSpec((1, bsz, tn), lambda l, j: (l, 0, j)),
        out_shape=jax.ShapeDtypeStruct((depth, bsz, n), F32),
        compiler_params=_cparams(("parallel", "parallel")),
        name="adaln_mod",
    )(c, ada_w, ada_b.reshape(depth, 1, n))


def _gdn_inproj_kernel(x_ref, g_ref, sc_ref, sh_ref, w_ref, wab_ref, o_ref, ab_ref, h_ref):
    @pl.when(pl.program_id(1) == 0)
    def _():
        h = _norm_mod(x_ref[...], g_ref[...], sc_ref[0], sh_ref[0]).astype(BF16)
        h_ref[...] = h
        ab_ref[...] = _dot(h, wab_ref[...])

    o_ref[...] = _dot(h_ref[...], w_ref[...]).astype(o_ref.dtype)


def _gdn_inproj(x2, g, mod, w_main, w_ab, seq):
    n, d = x2.shape
    nout = w_main.shape[1]
    tm, tn = ROW_TILE, 1024
    per_b = seq // tm
    return pl.pallas_call(
        _gdn_inproj_kernel,
        grid=(n // tm, nout // tn),
        in_specs=[
            pl.BlockSpec((tm, d), lambda i, j: (i, 0)),
            pl.BlockSpec((1, d), lambda i, j: (0, 0)),
            pl.BlockSpec((1, 1, d), lambda i, j: ((i // per_b) * N_MOD + 1, 0, 0)),
            pl.BlockSpec((1, 1, d), lambda i, j: ((i // per_b) * N_MOD + 0, 0, 0)),
            pl.BlockSpec((d, tn), lambda i, j: (0, j)),
            pl.BlockSpec((d, 128), lambda i, j: (0, 0)),
        ],
        out_specs=[
            pl.BlockSpec((tm, tn), lambda i, j: (i, j)),
            pl.BlockSpec((tm, 128), lambda i, j: (i, 0)),
        ],
        out_shape=[
            jax.ShapeDtypeStruct((n, nout), BF16),
            jax.ShapeDtypeStruct((n, 128), F32),
        ],
        scratch_shapes=[pltpu.VMEM((tm, d), BF16)],
        compiler_params=_cparams(("parallel", "arbitrary")),
        name="gdn_inproj",
    )(x2, g, mod, mod, w_main, w_ab)


def _gdn_core_kernel(alog_ref, dtb_ref, q_ref, k_ref, v_ref, gate_ref, ab_ref, cw_ref, ng_ref,
                     o_ref, xc_ref, p_ref, ou_ref, m_ref, r_ref, egl_ref):
    head = pl.program_id(1)
    seq = q_ref.shape[1]
    cl = GDN_CHUNK
    n_chunks = seq // cl

    def conv_slab(src_ref, which, s, first):
        if first:
            body = src_ref[0, 0:CONV_SLAB, :].astype(F32)
            ext = jnp.concatenate([jnp.zeros((CONV_HALO, GDN_HEAD_DIM), F32), body], axis=0)
            r0 = 0
        else:
            start = pl.multiple_of(s * CONV_SLAB - CONV_HALO, CONV_HALO)
            ext = src_ref[0, pl.ds(start, CONV_SLAB + CONV_HALO), :].astype(F32)
            r0 = pl.multiple_of(s * CONV_SLAB, CONV_SLAB)
        w = cw_ref[0, which]
        acc = ext[CONV_HALO:] * w[GDN_CONV - 1:GDN_CONV]
        for sh in range(1, GDN_CONV):
            tap = GDN_CONV - 1 - sh
            acc = acc + pltpu.roll(ext, sh, 0)[CONV_HALO:] * w[tap:tap + 1]
        y = _silu(acc)
        if which < 2:
            y = y * lax.rsqrt(jnp.sum(y * y, axis=-1, keepdims=True) + EPS)
        if which == 0:
            y = y * (GDN_HEAD_DIM ** -0.5)
        xc_ref[which, pl.ds(r0, CONV_SLAB), :] = y

    for which, src in enumerate((q_ref, k_ref, v_ref)):
        conv_slab(src, which, 0, True)

        def slab_body(s, carry, src=src, which=which):
            conv_slab(src, which, s, False)
            return carry

        lax.fori_loop(1, seq // CONV_SLAB, slab_body, 0)

    a_head = jnp.exp(jnp.full((1, 1), alog_ref[head], F32))
    dt_head = dtb_ref[head]
    row = lax.broadcasted_iota(jnp.int32, (cl, cl), 0)
    col = lax.broadcasted_iota(jnp.int32, (cl, cl), 1)
    tril = row >= col
    strict = row > col
    tril_b = tril.astype(F32).astype(BF16)
    eye = (row == col).astype(F32)
    lane = lax.broadcasted_iota(jnp.int32, (cl, 128), 1)

    def prep_chunk(c):
        r0 = pl.multiple_of(c * cl, cl)
        q = xc_ref[0, pl.ds(r0, cl), :]
        k = xc_ref[1, pl.ds(r0, cl), :]
        v = xc_ref[2, pl.ds(r0, cl), :]
        ab = ab_ref[0, pl.ds(r0, cl), :]
        a_col = jnp.sum(jnp.where(lane == head, ab, 0.0), axis=1, keepdims=True)
        b_col = jnp.sum(jnp.where(lane == head + GDN_HEADS, ab, 0.0), axis=1, keepdims=True)
        g = -a_head * _softplus(a_col + dt_head)
        beta = _sigmoid(b_col)

        g_hi, g_lo = _split(jnp.broadcast_to(g, (cl, 128)))
        gc = _dot(tril_b, g_hi) + _dot(tril_b, g_lo)
        gl = gc[cl - 1:cl, :]
        e_gc = jnp.exp(gc)
        e_rev = jnp.exp(gl - gc)
        e_gl = jnp.exp(gl)
        diff = gc[:, :cl] - gc.T[:cl, :]
        dec = jnp.exp(jnp.where(tril, diff, -jnp.inf))

        kb = k * beta
        k_b16 = k.astype(BF16)
        lmat = jnp.where(strict, _dot_nt(kb.astype(BF16), k_b16) * dec, 0.0)
        pw = -lmat
        tinv = eye + pw
        span = 2
        while span < cl:
            pw = _dot3(pw, pw)
            tinv = tinv + _dot3(tinv, pw)
            span *= 2

        rhs = jnp.concatenate([kb * e_gc, v * beta], axis=1).astype(BF16)
        t_hi, t_lo = _split(tinv)
        wu = (_dot(t_hi, rhs) + _dot(t_lo, rhs)).astype(BF16)
        attn = (_dot_nt(q.astype(BF16), k_b16) * dec).astype(BF16)
        aw = _dot(attn, wu)
        kw = _dot((k * e_rev).T.astype(BF16), wu)
        p_ref[c] = (q * e_gc - aw[:, :GDN_HEAD_DIM]).astype(BF16)
        ou_ref[c] = aw[:, GDN_HEAD_DIM:]
        m_ref[c] = (-kw[:, :GDN_HEAD_DIM]).astype(BF16)
        r_ref[c] = kw[:, GDN_HEAD_DIM:]
        egl_ref[c] = jnp.broadcast_to(e_gl, (8, 128))

    def prep_body(i, carry):
        for u in range(GDN_GROUP):
            prep_chunk(i * GDN_GROUP + u)
        return carry

    lax.fori_loop(0, n_chunks // GDN_GROUP, prep_body, 0)

    ng = ng_ref[...]

    def scan_body(c, s):
        r0 = pl.multiple_of(c * cl, cl)
        sb = s.astype(BF16)
        o = _dot(p_ref[c], sb) + ou_ref[c]
        s_new = egl_ref[c][0:1, :] * s + (_dot(m_ref[c], sb) + r_ref[c])
        rn = lax.rsqrt(jnp.mean(o * o, axis=-1, keepdims=True) + EPS)
        gt = gate_ref[0, pl.ds(r0, cl), :].astype(F32)
        o_ref[0, pl.ds(r0, cl), :] = ((o * rn) * ng * _silu(gt)).astype(o_ref.dtype)
        return s_new

    lax.fori_loop(0, n_chunks, scan_body, jnp.zeros((GDN_HEAD_DIM, GDN_HEAD_DIM), F32))


def _gdn_core(main, ab, cw, a_log, dt_bias, norm_g, bsz, seq):
    hd = GDN_HEAD_DIM
    nh = GDN_HEADS
    n_chunks = seq // GDN_CHUNK
    main3 = main.reshape(bsz, seq, main.shape[1])
    ab3 = ab.reshape(bsz, seq, 128)
    smem = pl.BlockSpec(memory_space=pltpu.SMEM)

    def col_spec(off):
        return pl.BlockSpec((1, seq, hd), lambda b, h, off=off: (b, 0, off + h))

    return pl.pallas_call(
        _gdn_core_kernel,
        grid=(bsz, nh),
        in_specs=[
            smem, smem,
            col_spec(0), col_spec(nh), col_spec(2 * nh), col_spec(3 * nh),
            pl.BlockSpec((1, seq, 128), lambda b, h: (b, 0, 0)),
            pl.BlockSpec((1, 3, GDN_CONV, hd), lambda b, h: (h, 0, 0, 0)),
            pl.BlockSpec((1, hd), lambda b, h: (0, 0)),
        ],
        out_specs=pl.BlockSpec((1, seq, hd), lambda b, h: (b, 0, h)),
        out_shape=jax.ShapeDtypeStruct((bsz, seq, nh * hd), BF16),
        scratch_shapes=[
            pltpu.VMEM((3, seq, hd), F32),
            pltpu.VMEM((n_chunks, GDN_CHUNK, hd), BF16),
            pltpu.VMEM((n_chunks, GDN_CHUNK, hd), F32),
            pltpu.VMEM((n_chunks, hd, hd), BF16),
            pltpu.VMEM((n_chunks, hd, hd), F32),
            pltpu.VMEM((n_chunks, 8, 128), F32),
        ],
        compiler_params=_cparams(("parallel", "parallel")),
        name="gdn_core",
    )(a_log, dt_bias, main3, main3, main3, main3, ab3, cw, norm_g)


def _outproj_kernel(a_ref, w_ref, x_ref, gt_ref, o_ref):
    o_ref[...] = x_ref[...] + gt_ref[0] * _dot(a_ref[...], w_ref[...])


def _outproj_residual(a2, w, x2, mod, seq):
    n, d = x2.shape
    k = a2.shape[1]
    tm = ROW_TILE
    per_b = seq // tm
    return pl.pallas_call(
        _outproj_kernel,
        grid=(n // tm,),
        in_specs=[
            pl.BlockSpec((tm, k), lambda i: (i, 0)),
            pl.BlockSpec((k, d), lambda i: (0, 0)),
            pl.BlockSpec((tm, d), lambda i: (i, 0)),
            pl.BlockSpec((1, 1, d), lambda i: ((i // per_b) * N_MOD + 2, 0, 0)),
        ],
        out_specs=pl.BlockSpec((tm, d), lambda i: (i, 0)),
        out_shape=jax.ShapeDtypeStruct((n, d), F32),
        compiler_params=_cparams(("parallel",)),
        name="outproj_residual",
    )(a2, w, x2, mod)


def _mla_proj_kernel(x_ref, g_ref, sc_ref, sh_ref, pos_ref, tab_ref, win_ref, qg_ref, kvg_ref,
                     wq_ref, wqs_ref, wkn_ref, wv_ref, q_ref, kn_ref, kr_ref, v_ref):
    h = _norm_mod(x_ref[...], g_ref[...], sc_ref[0], sh_ref[0]).astype(BF16)
    proj = _dot(h, win_ref[...])
    qr, kvr = MLA_Q_RANK, MLA_KV_RANK
    c_q = proj[:, :qr]
    c_kv = proj[:, qr:qr + kvr]
    kr_a = proj[:, qr + kvr:qr + kvr + 128]
    kr_b = proj[:, qr + kvr + 128:qr + kvr + 256]

    def rms(t, gain):
        return ((t * lax.rsqrt(jnp.mean(t * t, axis=-1, keepdims=True) + EPS)) * gain).astype(BF16)

    cqn = rms(c_q, qg_ref[...])
    ckvn = rms(c_kv, kvg_ref[...])

    ang = pos_ref[...].astype(F32) * tab_ref[0:1, :]
    ct = jnp.cos(ang) * tab_ref[1:2, :]
    st = jnp.sin(ang) * tab_ref[2:3, :]
    kr_ref[...] = (kr_a * ct + kr_b * st).astype(kr_ref.dtype)

    scale = (MLA_NOPE + MLA_ROPE) ** -0.5
    qf = _dot(cqn, wq_ref[...])
    qs = _dot(cqn, wqs_ref[...])
    for hh in range(MLA_HEADS):
        base = hh * MLA_QK_PAD
        q_ref[:, base:base + 128] = (qf[:, base:base + 128] * scale).astype(q_ref.dtype)
        rot = qf[:, base + 128:base + 256] * ct + qs[:, hh * 128:(hh + 1) * 128] * st
        q_ref[:, base + 128:base + 256] = (rot * scale).astype(q_ref.dtype)

    kn_ref[...] = _dot(ckvn, wkn_ref[...]).astype(kn_ref.dtype)
    v_ref[...] = _dot(ckvn, wv_ref[...]).astype(v_ref.dtype)


def _mla_proj(x2, g, mod, pos2, tab, w_in, qg, kvg, wq, wqs, wkn, wv, seq):
    n, d = x2.shape
    tm = ROW_TILE
    per_b = seq // tm
    full = lambda a: pl.BlockSpec(a.shape, lambda i: (0,) * a.ndim)
    hq = MLA_HEADS * MLA_QK_PAD
    hv = MLA_HEADS * MLA_V
    return pl.pallas_call(
        _mla_proj_kernel,
        grid=(n // tm,),
        in_specs=[
            pl.BlockSpec((tm, d), lambda i: (i, 0)),
            full(g),
            pl.BlockSpec((1, 1, d), lambda i: ((i // per_b) * N_MOD + 1, 0, 0)),
            pl.BlockSpec((1, 1, d), lambda i: ((i // per_b) * N_MOD + 0, 0, 0)),
            pl.BlockSpec((tm, 1), lambda i: (i, 0)),
            full(tab), full(w_in), full(qg), full(kvg), full(wq), full(wqs), full(wkn), full(wv),
        ],
        out_specs=[
            pl.BlockSpec((tm, hq), lambda i: (i, 0)),
            pl.BlockSpec((tm, hv), lambda i: (i, 0)),
            pl.BlockSpec((tm, 128), lambda i: (i, 0)),
            pl.BlockSpec((tm, hv), lambda i: (i, 0)),
        ],
        out_shape=[
            jax.ShapeDtypeStruct((n, hq), BF16),
            jax.ShapeDtypeStruct((n, hv), BF16),
            jax.ShapeDtypeStruct((n, 128), BF16),
            jax.ShapeDtypeStruct((n, hv), BF16),
        ],
        compiler_params=_cparams(("parallel",)),
        name="mla_proj",
    )(x2, g, mod, mod, pos2, tab, w_in, qg, kvg, wq, wqs, wkn, wv)


def _flash_kernel(q_ref, kn_ref, kr_ref, v_ref, o_ref, m_ref, l_ref, acc_ref):
    qi = pl.program_id(2)
    kj = pl.program_id(3)

    @pl.when(kj == 0)
    def _():
        m_ref[...] = jnp.full(m_ref.shape, -jnp.inf, F32)
        l_ref[...] = jnp.zeros(l_ref.shape, F32)
        acc_ref[...] = jnp.zeros(acc_ref.shape, F32)

    def step(masked):
        kcat = jnp.concatenate([kn_ref[0], kr_ref[0]], axis=1)
        s = _dot_nt(q_ref[0], kcat)
        if masked:
            row = lax.broadcasted_iota(jnp.int32, s.shape, 0)
            col = lax.broadcasted_iota(jnp.int32, s.shape, 1)
            s = jnp.where(col <= row, s, -jnp.inf)
        m_prev = m_ref[...]
        m_new = jnp.maximum(m_prev, jnp.max(s, axis=1, keepdims=True))
        alpha = jnp.exp(m_prev - m_new)
        p = jnp.exp(s - m_new)
        l_ref[...] = alpha * l_ref[...] + jnp.sum(p, axis=1, keepdims=True)
        acc_ref[...] = alpha * acc_ref[...] + _dot(p.astype(BF16), v_ref[0])
        m_ref[...] = m_new

    @pl.when(kj < qi)
    def _():
        step(False)

    @pl.when(kj == qi)
    def _():
        step(True)
        o_ref[0] = (acc_ref[...] / l_ref[...]).astype(o_ref.dtype)


def _flash_attention(qcat, kn, kr, v, bsz, seq):
    nh = MLA_HEADS
    blk = ATTN_BLOCK
    nb = seq // blk
    q3 = qcat.reshape(bsz, seq, nh * MLA_QK_PAD)
    kn3 = kn.reshape(bsz, seq, nh * MLA_NOPE)
    kr3 = kr.reshape(bsz, seq, 128)
    v3 = v.reshape(bsz, seq, nh * MLA_V)
    return pl.pallas_call(
        _flash_kernel,
        grid=(bsz, nh, nb, nb),
        in_specs=[
            pl.BlockSpec((1, blk, MLA_QK_PAD), lambda b, h, i, j: (b, i, h)),
            pl.BlockSpec((1, blk, MLA_NOPE), lambda b, h, i, j: (b, jnp.minimum(j, i), h)),
            pl.BlockSpec((1, blk, 128), lambda b, h, i, j: (b, jnp.minimum(j, i), 0)),
            pl.BlockSpec((1, blk, MLA_V), lambda b, h, i, j: (b, jnp.minimum(j, i), h)),
        ],
        out_specs=pl.BlockSpec((1, blk, MLA_V), lambda b, h, i, j: (b, i, h)),
        out_shape=jax.ShapeDtypeStruct((bsz, seq, nh * MLA_V), BF16),
        scratch_shapes=[
            pltpu.VMEM((blk, 1), F32),
            pltpu.VMEM((blk, 1), F32),
            pltpu.VMEM((blk, MLA_V), F32),
        ],
        compiler_params=_cparams(("parallel", "parallel", "parallel", "arbitrary")),
        name="mla_flash",
    )(q3, kn3, kr3, v3)


def _ffn_kernel(x_ref, g_ref, sc_ref, sh_ref, gt_ref, wg_ref, wu_ref, wd_ref, fg_ref, o_ref,
                h_ref, acc_ref, *, final):
    f = pl.program_id(1)

    @pl.when(f == 0)
    def _():
        h_ref[...] = _norm_mod(x_ref[...], g_ref[...], sc_ref[0], sh_ref[0]).astype(BF16)
        acc_ref[...] = jnp.zeros(acc_ref.shape, F32)

    hb = h_ref[...]
    act = (_silu(_dot(hb, wg_ref[...])) * _dot(hb, wu_ref[...])).astype(BF16)
    acc_ref[...] += _dot(act, wd_ref[...])

    @pl.when(f == pl.num_programs(1) - 1)
    def _():
        xn = x_ref[...] + gt_ref[0] * acc_ref[...]
        if final:
            xn = (xn * lax.rsqrt(jnp.mean(xn * xn, axis=-1, keepdims=True) + EPS)) * fg_ref[...]
        o_ref[...] = xn


def _ffn(x2, g, mod, wg, wu, wd, fg, seq, final):
    n, d = x2.shape
    dff = wg.shape[1]
    tm, tf = ROW_TILE, dff // 2
    per_b = seq // tm
    mod_spec = lambda k: pl.BlockSpec((1, 1, d), lambda i, f, k=k: ((i // per_b) * N_MOD + k, 0, 0))
    return pl.pallas_call(
        functools.partial(_ffn_kernel, final=final),
        grid=(n // tm, dff // tf),
        in_specs=[
            pl.BlockSpec((tm, d), lambda i, f: (i, 0)),
            pl.BlockSpec((1, d), lambda i, f: (0, 0)),
            mod_spec(4), mod_spec(3), mod_spec(5),
            pl.BlockSpec((d, tf), lambda i, f: (0, f)),
            pl.BlockSpec((d, tf), lambda i, f: (0, f)),
            pl.BlockSpec((tf, d), lambda i, f: (f, 0)),
            pl.BlockSpec((1, d), lambda i, f: (0, 0)),
        ],
        out_specs=pl.BlockSpec((tm, d), lambda i, f: (i, 0)),
        out_shape=jax.ShapeDtypeStruct((n, d), F32),
        scratch_shapes=[pltpu.VMEM((tm, d), BF16), pltpu.VMEM((tm, d), F32)],
        compiler_params=_cparams(("parallel", "arbitrary")),
        name="ffn_final" if final else "ffn",
    )(x2, g, mod, mod, mod, wg, wu, wd, fg)


def _gdn_weights(w_in, conv_w):
    main_cols = 2 * GDN_HEADS * GDN_HEAD_DIM + 2 * GDN_HEADS * GDN_HEAD_DIM
    w_main = w_in[:, :main_cols].astype(BF16)
    w_ab = jnp.pad(w_in[:, main_cols:], ((0, 0), (0, 128 - 2 * GDN_HEADS))).astype(BF16)
    cw = conv_w.T.reshape(GDN_CONV, 3, GDN_HEADS, GDN_HEAD_DIM).transpose(2, 1, 0, 3)
    return w_main, w_ab, cw


def _mla_weights(w_in, w_uq, w_ukv):
    qr, kvr, half = MLA_Q_RANK, MLA_KV_RANK, MLA_ROPE // 2
    d = w_in.shape[0]
    rope = w_in[:, qr + kvr:]
    z = jnp.zeros((d, 128 - MLA_ROPE), w_in.dtype)
    w_in_ext = jnp.concatenate(
        [w_in[:, :qr + kvr], rope, z, rope[:, half:], rope[:, :half], z], axis=1).astype(BF16)
    uq = w_uq.reshape(qr, MLA_HEADS, MLA_NOPE + MLA_ROPE)
    nope, r = uq[..., :MLA_NOPE], uq[..., MLA_NOPE:]
    zq = jnp.zeros((qr, MLA_HEADS, 128 - MLA_ROPE), w_uq.dtype)
    wq = jnp.concatenate([nope, r, zq], axis=-1).reshape(qr, MLA_HEADS * MLA_QK_PAD).astype(BF16)
    wqs = jnp.concatenate([r[..., half:], r[..., :half], zq], axis=-1).reshape(
        qr, MLA_HEADS * 128).astype(BF16)
    ukv = w_ukv.reshape(kvr, MLA_HEADS, MLA_NOPE + MLA_V)
    wkn = ukv[..., :MLA_NOPE].reshape(kvr, MLA_HEADS * MLA_NOPE).astype(BF16)
    wv = ukv[..., MLA_NOPE:].reshape(kvr, MLA_HEADS * MLA_V).astype(BF16)
    return w_in_ext, wq, wqs, wkn, wv


def _rope_table():
    half = MLA_ROPE // 2
    inv_freq = ROPE_THETA ** (-jnp.arange(0, MLA_ROPE, 2, dtype=F32) / MLA_ROPE)
    z = jnp.zeros((128 - MLA_ROPE,), F32)
    ones = jnp.ones((half,), F32)
    freq = jnp.concatenate([inv_freq, inv_freq, z])
    cmask = jnp.concatenate([ones, ones, z])
    ssign = jnp.concatenate([-ones, ones, z])
    pad = jnp.zeros((5, 128), F32)
    return jnp.concatenate([jnp.stack([freq, cmask, ssign]), pad], axis=0)


def kernel(x, c, positions, ada_w, ada_b, norm_mix_g, norm_ffn_g, gdn_w_in, gdn_conv_w, gdn_a_log,
           gdn_dt_bias, gdn_norm_g, gdn_w_out, mla_w_in, mla_q_norm_g, mla_kv_norm_g, mla_w_uq,
           mla_w_ukv, mla_w_out, ffn_w_gate, ffn_w_up, ffn_w_down, final_norm_g):
    bsz, seq, d = x.shape
    depth = ada_w.shape[0]
    n_mixers = 2
    x2 = x.reshape(bsz * seq, d)
    pos2 = positions.reshape(bsz * seq, 1)
    tab = _rope_table()

    mod_all = _adaln(c, ada_w, ada_b).reshape(depth, bsz * N_MOD, 1, d)

    for layer in range(depth):
        mod = mod_all[layer]
        j = layer // n_mixers
        g_mix = norm_mix_g[layer].reshape(1, d)
        if layer % n_mixers == 0:
            w_main, w_ab, cw = _gdn_weights(gdn_w_in[j], gdn_conv_w[j])
            main, ab = _gdn_inproj(x2, g_mix, mod, w_main, w_ab, seq)
            o = _gdn_core(main, ab, cw, gdn_a_log[j], gdn_dt_bias[j],
                          gdn_norm_g[j].reshape(1, GDN_HEAD_DIM), bsz, seq)
            x2 = _outproj_residual(o.reshape(bsz * seq, -1), gdn_w_out[j].astype(BF16), x2, mod, seq)
        else:
            w_in_ext, wq, wqs, wkn, wv = _mla_weights(mla_w_in[j], mla_w_uq[j], mla_w_ukv[j])
            qcat, kn, kr, v = _mla_proj(
                x2, g_mix, mod, pos2, tab, w_in_ext,
                mla_q_norm_g[j].reshape(1, -1), mla_kv_norm_g[j].reshape(1, -1),
                wq, wqs, wkn, wv, seq)
            o = _flash_attention(qcat, kn, kr, v, bsz, seq)
            x2 = _outproj_residual(o.reshape(bsz * seq, -1), mla_w_out[j].astype(BF16), x2, mod, seq)

        x2 = _ffn(x2, norm_ffn_g[layer].reshape(1, d), mod,
                  ffn_w_gate[layer].astype(BF16), ffn_w_up[layer].astype(BF16),
                  ffn_w_down[layer].astype(BF16), final_norm_g.reshape(1, d), seq,
                  final=(layer == depth - 1))

    return x2.reshape(bsz, seq, d)
```

```python
import functools

import jax
import jax.numpy as jnp
from jax import lax
from jax.experimental import pallas as pl
from jax.experimental.pallas import tpu as pltpu

F32 = jnp.float32
BF16 = jnp.bfloat16

D_MODEL = 1024
N_MOD = 6
EPS = 1e-6

GDN_HEADS = 8
GDN_HEAD_DIM = 128
GDN_CONV = 4
GDN_CHUNK = 64
GDN_HEADS_PER_STEP = 4
GDN_GROUP = 16
CONV_SLAB = 256
CONV_HALO = 16

MLA_HEADS = 8
MLA_NOPE = 128
MLA_ROPE = 64
MLA_V = 128
MLA_Q_RANK = 384
MLA_KV_RANK = 256
MLA_QK_PAD = 256
ROPE_THETA = 10000.0

ROW_TILE = 512
ATTN_BLOCK = 512
VMEM_LIMIT = 56 * 1024 * 1024


def _cparams(sem, vmem=VMEM_LIMIT):
    return pltpu.CompilerParams(dimension_semantics=sem, vmem_limit_bytes=vmem)


def _sigmoid(x):
    return 1.0 / (1.0 + jnp.exp(-x))


def _silu(x):
    return x * _sigmoid(x)


def _softplus(x):
    return jnp.maximum(x, 0.0) + jnp.log(1.0 + jnp.exp(-jnp.abs(x)))


def _dot(a, b):
    return jnp.dot(a, b, preferred_element_type=F32)


def _dot_nt(a, b):
    return lax.dot_general(a, b, (((1,), (1,)), ((), ())), preferred_element_type=F32)


def _split(a):
    hi = a.astype(BF16)
    lo = (a - hi.astype(F32)).astype(BF16)
    return hi, lo


def _norm_mod(x, g, scale, shift):
    r = lax.rsqrt(jnp.mean(x * x, axis=-1, keepdims=True) + EPS)
    return (x * r) * g * (1.0 + scale) + shift


def _adaln_kernel(c_ref, w_ref, b_ref, o_ref):
    c_act = _silu(c_ref[...]).astype(BF16)
    o_ref[0] = _dot(c_act, w_ref[0].astype(BF16)) + b_ref[0]


def _adaln(c, ada_w, ada_b):
    depth, d, n = ada_w.shape
    bsz = c.shape[0]
    tn = 1536
    return pl.pallas_call(
        _adaln_kernel,
        grid=(depth, n // tn),
        in_specs=[
            pl.BlockSpec((bsz, d), lambda l, j: (0, 0)),
            pl.BlockSpec((1, d, tn), lambda l, j: (l, 0, j)),
            pl.BlockSpec((1, 1, tn), lambda l, j: (l, 0, j)),
        ],
        out_specs=pl.BlockSpec((1, bsz, tn), lambda l, j: (l, 0, j)),
        out_shape=jax.ShapeDtypeStruct((depth, bsz, n), F32),
        compiler_params=_cparams(("parallel", "parallel")),
        name="adaln_mod",
    )(c, ada_w, ada_b.reshape(depth, 1, n))


def _gdn_inproj_kernel(x_ref, g_ref, sc_ref, sh_ref, w_ref, wab_ref, o_ref, ab_ref, h_ref):
    @pl.when(pl.program_id(1) == 0)
    def _():
        h = _norm_mod(x_ref[...], g_ref[...], sc_ref[0], sh_ref[0]).astype(BF16)
        h_ref[...] = h
        ab_ref[...] = _dot(h, wab_ref[...])

    o_ref[...] = _dot(h_ref[...], w_ref[...]).astype(o_ref.dtype)


def _gdn_inproj(x2, g, mod, w_main, w_ab, seq):
    n, d = x2.shape
    nout = w_main.shape[1]
    tm, tn = ROW_TILE, 1024
    per_b = seq // tm
    return pl.pallas_call(
        _gdn_inproj_kernel,
        grid=(n // tm, nout // tn),
        in_specs=[
            pl.BlockSpec((tm, d), lambda i, j: (i, 0)),
            pl.BlockSpec((1, d), lambda i, j: (0, 0)),
            pl.BlockSpec((1, 1, d), lambda i, j: ((i // per_b) * N_MOD + 1, 0, 0)),
            pl.BlockSpec((1, 1, d), lambda i, j: ((i // per_b) * N_MOD + 0, 0, 0)),
            pl.BlockSpec((d, tn), lambda i, j: (0, j)),
            pl.BlockSpec((d, 128), lambda i, j: (0, 0)),
        ],
        out_specs=[
            pl.BlockSpec((tm, tn), lambda i, j: (i, j)),
            pl.BlockSpec((tm, 128), lambda i, j: (i, 0)),
        ],
        out_shape=[
            jax.ShapeDtypeStruct((n, nout), BF16),
            jax.ShapeDtypeStruct((n, 128), F32),
        ],
        scratch_shapes=[pltpu.VMEM((tm, d), BF16)],
        compiler_params=_cparams(("parallel", "arbitrary")),
        name="gdn_inproj",
    )(x2, g, mod, mod, w_main, w_ab)


def _run_interleaved(gens):
    while gens:
        alive = []
        for gen in gens:
            try:
                next(gen)
                alive.append(gen)
            except StopIteration:
                pass
        gens = alive


def _gdn_core_kernel(alog_ref, dtb_ref, q_ref, k_ref, v_ref, gate_ref, ab_ref, cw_ref, ng_ref,
                     o_ref, xc_ref, p_ref, ou_ref, m_ref, r_ref, egl_ref):
    hps = GDN_HEADS_PER_STEP
    hd = GDN_HEAD_DIM
    head0 = pl.program_id(1) * hps
    seq = q_ref.shape[1]
    cl = GDN_CHUNK
    n_chunks = seq // cl

    row = lax.broadcasted_iota(jnp.int32, (cl, cl), 0)
    col = lax.broadcasted_iota(jnp.int32, (cl, cl), 1)
    tril = row >= col
    strict = row > col
    tril_b = tril.astype(F32).astype(BF16)
    eye = (row == col).astype(F32)
    lane = lax.broadcasted_iota(jnp.int32, (cl, 128), 1)

    for hh in range(hps):
        head = head0 + hh
        lanes = slice(hh * hd, (hh + 1) * hd)

        def conv_slab(src_ref, which, s, first, hh=hh, lanes=lanes):
            if first:
                body = src_ref[0, 0:CONV_SLAB, lanes].astype(F32)
                ext = jnp.concatenate([jnp.zeros((CONV_HALO, hd), F32), body], axis=0)
                r0 = 0
            else:
                start = pl.multiple_of(s * CONV_SLAB - CONV_HALO, CONV_HALO)
                ext = src_ref[0, pl.ds(start, CONV_SLAB + CONV_HALO), lanes].astype(F32)
                r0 = pl.multiple_of(s * CONV_SLAB, CONV_SLAB)
            w = cw_ref[hh, which]
            acc = ext[CONV_HALO:] * w[GDN_CONV - 1:GDN_CONV]
            for sh in range(1, GDN_CONV):
                tap = GDN_CONV - 1 - sh
                acc = acc + pltpu.roll(ext, sh, 0)[CONV_HALO:] * w[tap:tap + 1]
            y = _silu(acc)
            if which < 2:
                y = y * lax.rsqrt(jnp.sum(y * y, axis=-1, keepdims=True) + EPS)
            if which == 0:
                y = y * (hd ** -0.5)
            xc_ref[which, pl.ds(r0, CONV_SLAB), :] = y

        for which, src in enumerate((q_ref, k_ref, v_ref)):
            conv_slab(src, which, 0, True)

            def slab_body(s, carry, src=src, which=which, conv_slab=conv_slab):
                conv_slab(src, which, s, False)
                return carry

            lax.fori_loop(1, seq // CONV_SLAB, slab_body, 0)

        a_head = jnp.exp(jnp.full((1, 1), alog_ref[head], F32))
        dt_head = dtb_ref[head]

        def prep_chunk(c, hh=hh, head=head, a_head=a_head, dt_head=dt_head):
            r0 = pl.multiple_of(c * cl, cl)
            q = xc_ref[0, pl.ds(r0, cl), :]
            k = xc_ref[1, pl.ds(r0, cl), :]
            v = xc_ref[2, pl.ds(r0, cl), :]
            ab = ab_ref[0, pl.ds(r0, cl), :]
            yield
            a_col = jnp.sum(jnp.where(lane == head, ab, 0.0), axis=1, keepdims=True)
            b_col = jnp.sum(jnp.where(lane == head + GDN_HEADS, ab, 0.0), axis=1, keepdims=True)
            g = -a_head * _softplus(a_col + dt_head)
            beta = _sigmoid(b_col)
            g_hi, g_lo = _split(jnp.broadcast_to(g, (cl, 128)))
            gc = _dot(tril_b, g_hi) + _dot(tril_b, g_lo)
            kb = k * beta
            k_b16 = k.astype(BF16)
            kk = _dot_nt(kb.astype(BF16), k_b16)
            qk = _dot_nt(q.astype(BF16), k_b16)
            yield
            gl = gc[cl - 1:cl, :]
            e_gc = jnp.exp(gc)
            e_rev = jnp.exp(gl - gc)
            e_gl = jnp.exp(gl)
            diff = gc[:, :cl] - gc.T[:cl, :]
            dec = jnp.exp(jnp.where(tril, diff, -jnp.inf))
            pw = -jnp.where(strict, kk * dec, 0.0)
            tinv = eye + pw
            attn = (qk * dec).astype(BF16)
            rhs = jnp.concatenate([kb * e_gc, v * beta], axis=1).astype(BF16)
            kdt = (k * e_rev).T.astype(BF16)
            qd = q * e_gc
            span = 2
            while span < cl:
                pw_b = pw.astype(BF16)
                pw = _dot(pw_b, pw_b)
                yield
                tinv = tinv + _dot(tinv.astype(BF16), pw.astype(BF16))
                span *= 2
            yield
            wu = _dot(tinv.astype(BF16), rhs).astype(BF16)
            yield
            aw = _dot(attn, wu)
            kw = _dot(kdt, wu)
            yield
            p_ref[hh, c] = (qd - aw[:, :hd]).astype(BF16)
            ou_ref[hh, c] = aw[:, hd:]
            m_ref[hh, c] = (-kw[:, :hd]).astype(BF16)
            r_ref[hh, c] = kw[:, hd:]
            egl_ref[hh, c] = jnp.broadcast_to(e_gl, (8, 128))

        def prep_body(i, carry, prep_chunk=prep_chunk):
            _run_interleaved([prep_chunk(i * GDN_GROUP + u) for u in range(GDN_GROUP)])
            return carry

        lax.fori_loop(0, n_chunks // GDN_GROUP, prep_body, 0)

    ng = ng_ref[...]

    def scan_body(c, states):
        r0 = pl.multiple_of(c * cl, cl)
        new_states = [None] * hps

        def head_step(hh):
            s = states[hh]
            p = p_ref[hh, c]
            m = m_ref[hh, c]
            ou = ou_ref[hh, c]
            r = r_ref[hh, c]
            eg = egl_ref[hh, c][0:1, :]
            gt = gate_ref[0, pl.ds(r0, cl), hh * hd:(hh + 1) * hd].astype(F32)
            yield
            sb = s.astype(BF16)
            o = _dot(p, sb) + ou
            new_states[hh] = eg * s + (_dot(m, sb) + r)
            rn = lax.rsqrt(jnp.mean(o * o, axis=-1, keepdims=True) + EPS)
            y = ((o * rn) * ng * _silu(gt)).astype(o_ref.dtype)
            yield
            o_ref[0, pl.ds(r0, cl), hh * hd:(hh + 1) * hd] = y

        _run_interleaved([head_step(hh) for hh in range(hps)])
        return tuple(new_states)

    init = tuple(jnp.zeros((hd, hd), F32) for _ in range(hps))
    lax.fori_loop(0, n_chunks, scan_body, init)


def _gdn_core(main, ab, cw, a_log, dt_bias, norm_g, bsz, seq):
    hd = GDN_HEAD_DIM
    nh = GDN_HEADS
    hps = GDN_HEADS_PER_STEP
    groups = nh // hps
    n_chunks = seq // GDN_CHUNK
    main3 = main.reshape(bsz, seq, main.shape[1])
    ab3 = ab.reshape(bsz, seq, 128)
    smem = pl.BlockSpec(memory_space=pltpu.SMEM)

    def col_spec(which):
        return pl.BlockSpec((1, seq, hps * hd), lambda b, h, which=which: (b, 0, which * groups + h))

    return pl.pallas_call(
        _gdn_core_kernel,
        grid=(bsz, groups),
        in_specs=[
            smem, smem,
            col_spec(0), col_spec(1), col_spec(2), col_spec(3),
            pl.BlockSpec((1, seq, 128), lambda b, h: (b, 0, 0)),
            pl.BlockSpec((hps, 3, GDN_CONV, hd), lambda b, h: (h, 0, 0, 0)),
            pl.BlockSpec((1, hd), lambda b, h: (0, 0)),
        ],
        out_specs=pl.BlockSpec((1, seq, hps * hd), lambda b, h: (b, 0, h)),
        out_shape=jax.ShapeDtypeStruct((bsz, seq, nh * hd), BF16),
        scratch_shapes=[
            pltpu.VMEM((3, seq, hd), F32),
            pltpu.VMEM((hps, n_chunks, GDN_CHUNK, hd), BF16),
            pltpu.VMEM((hps, n_chunks, GDN_CHUNK, hd), F32),
            pltpu.VMEM((hps, n_chunks, hd, hd), BF16),
            pltpu.VMEM((hps, n_chunks, hd, hd), F32),
            pltpu.VMEM((hps, n_chunks, 8, 128), F32),
        ],
        compiler_params=_cparams(("parallel", "parallel")),
        name="gdn_core",
    )(a_log, dt_bias, main3, main3, main3, main3, ab3, cw, norm_g)


def _outproj_kernel(a_ref, w_ref, x_ref, gt_ref, o_ref):
    o_ref[...] = x_ref[...] + gt_ref[0] * _dot(a_ref[...], w_ref[...])


def _outproj_residual(a2, w, x2, mod, seq):
    n, d = x2.shape
    k = a2.shape[1]
    tm = ROW_TILE
    per_b = seq // tm
    return pl.pallas_call(
        _outproj_kernel,
        grid=(n // tm,),
        in_specs=[
            pl.BlockSpec((tm, k), lambda i: (i, 0)),
            pl.BlockSpec((k, d), lambda i: (0, 0)),
            pl.BlockSpec((tm, d), lambda i: (i, 0)),
            pl.BlockSpec((1, 1, d), lambda i: ((i // per_b) * N_MOD + 2, 0, 0)),
        ],
        out_specs=pl.BlockSpec((tm, d), lambda i: (i, 0)),
        out_shape=jax.ShapeDtypeStruct((n, d), F32),
        compiler_params=_cparams(("parallel",)),
        name="outproj_residual",
    )(a2, w, x2, mod)


def _mla_proj_kernel(x_ref, g_ref, sc_ref, sh_ref, pos_ref, tab_ref, win_ref, qg_ref, kvg_ref,
                     wq_ref, wqs_ref, wkn_ref, wv_ref, q_ref, kn_ref, kr_ref, v_ref):
    h = _norm_mod(x_ref[...], g_ref[...], sc_ref[0], sh_ref[0]).astype(BF16)
    proj = _dot(h, win_ref[...])
    qr, kvr = MLA_Q_RANK, MLA_KV_RANK
    c_q = proj[:, :qr]
    c_kv = proj[:, qr:qr + kvr]
    kr_a = proj[:, qr + kvr:qr + kvr + 128]
    kr_b = proj[:, qr + kvr + 128:qr + kvr + 256]

    def rms(t, gain):
        return ((t * lax.rsqrt(jnp.mean(t * t, axis=-1, keepdims=True) + EPS)) * gain).astype(BF16)

    cqn = rms(c_q, qg_ref[...])
    ckvn = rms(c_kv, kvg_ref[...])

    ang = pos_ref[...].astype(F32) * tab_ref[0:1, :]
    ct = jnp.cos(ang) * tab_ref[1:2, :]
    st = jnp.sin(ang) * tab_ref[2:3, :]
    kr_ref[...] = (kr_a * ct + kr_b * st).astype(kr_ref.dtype)

    scale = (MLA_NOPE + MLA_ROPE) ** -0.5
    qf = _dot(cqn, wq_ref[...])
    qs = _dot(cqn, wqs_ref[...])
    for hh in range(MLA_HEADS):
        base = hh * MLA_QK_PAD
        q_ref[:, base:base + 128] = (qf[:, base:base + 128] * scale).astype(q_ref.dtype)
        rot = qf[:, base + 128:base + 256] * ct + qs[:, hh * 128:(hh + 1) * 128] * st
        q_ref[:, base + 128:base + 256] = (rot * scale).astype(q_ref.dtype)

    kn_ref[...] = _dot(ckvn, wkn_ref[...]).astype(kn_ref.dtype)
    v_ref[...] = _dot(ckvn, wv_ref[...]).astype(v_ref.dtype)


def _mla_proj(x2, g, mod, pos2, tab, w_in, qg, kvg, wq, wqs, wkn, wv, seq):
    n, d = x2.shape
    tm = ROW_TILE
    per_b = seq // tm
    full = lambda a: pl.BlockSpec(a.shape, lambda i: (0,) * a.ndim)
    hq = MLA_HEADS * MLA_QK_PAD
    hv = MLA_HEADS * MLA_V
    return pl.pallas_call(
        _mla_proj_kernel,
        grid=(n // tm,),
        in_specs=[
            pl.BlockSpec((tm, d), lambda i: (i, 0)),
            full(g),
            pl.BlockSpec((1, 1, d), lambda i: ((i // per_b) * N_MOD + 1, 0, 0)),
            pl.BlockSpec((1, 1, d), lambda i: ((i // per_b) * N_MOD + 0, 0, 0)),
            pl.BlockSpec((tm, 1), lambda i: (i, 0)),
            full(tab), full(w_in), full(qg), full(kvg), full(wq), full(wqs), full(wkn), full(wv),
        ],
        out_specs=[
            pl.BlockSpec((tm, hq), lambda i: (i, 0)),
            pl.BlockSpec((tm, hv), lambda i: (i, 0)),
            pl.BlockSpec((tm, 128), lambda i: (i, 0)),
            pl.BlockSpec((tm, hv), lambda i: (i, 0)),
        ],
        out_shape=[
            jax.ShapeDtypeStruct((n, hq), BF16),
            jax.ShapeDtypeStruct((n, hv), BF16),
            jax.ShapeDtypeStruct((n, 128), BF16),
            jax.ShapeDtypeStruct((n, hv), BF16),
        ],
        compiler_params=_cparams(("parallel",)),
        name="mla_proj",
    )(x2, g, mod, mod, pos2, tab, w_in, qg, kvg, wq, wqs, wkn, wv)


def _flash_kernel(q_ref, kn_ref, kr_ref, v_ref, o_ref, m_ref, l_ref, acc_ref):
    qi = pl.program_id(2)
    kj = pl.program_id(3)

    @pl.when(kj == 0)
    def _():
        m_ref[...] = jnp.full(m_ref.shape, -jnp.inf, F32)
        l_ref[...] = jnp.zeros(l_ref.shape, F32)
        acc_ref[...] = jnp.zeros(acc_ref.shape, F32)

    def step(masked):
        kcat = jnp.concatenate([kn_ref[0], kr_ref[0]], axis=1)
        s = _dot_nt(q_ref[0], kcat)
        if masked:
            row = lax.broadcasted_iota(jnp.int32, s.shape, 0)
            col = lax.broadcasted_iota(jnp.int32, s.shape, 1)
            s = jnp.where(col <= row, s, -jnp.inf)
        m_prev = m_ref[...]
        m_new = jnp.maximum(m_prev, jnp.max(s, axis=1, keepdims=True))
        alpha = jnp.exp(m_prev - m_new)
        p = jnp.exp(s - m_new)
        l_ref[...] = alpha * l_ref[...] + jnp.sum(p, axis=1, keepdims=True)
        acc_ref[...] = alpha * acc_ref[...] + _dot(p.astype(BF16), v_ref[0])
        m_ref[...] = m_new

    @pl.when(kj < qi)
    def _():
        step(False)

    @pl.when(kj == qi)
    def _():
        step(True)
        o_ref[0] = (acc_ref[...] / l_ref[...]).astype(o_ref.dtype)


def _flash_attention(qcat, kn, kr, v, bsz, seq):
    nh = MLA_HEADS
    blk = ATTN_BLOCK
    nb = seq // blk
    q3 = qcat.reshape(bsz, seq, nh * MLA_QK_PAD)
    kn3 = kn.reshape(bsz, seq, nh * MLA_NOPE)
    kr3 = kr.reshape(bsz, seq, 128)
    v3 = v.reshape(bsz, seq, nh * MLA_V)
    return pl.pallas_call(
        _flash_kernel,
        grid=(bsz, nh, nb, nb),
        in_specs=[
            pl.BlockSpec((1, blk, MLA_QK_PAD), lambda b, h, i, j: (b, i, h)),
            pl.BlockSpec((1, blk, MLA_NOPE), lambda b, h, i, j: (b, jnp.minimum(j, i), h)),
            pl.BlockSpec((1, blk, 128), lambda b, h, i, j: (b, jnp.minimum(j, i), 0)),
            pl.BlockSpec((1, blk, MLA_V), lambda b, h, i, j: (b, jnp.minimum(j, i), h)),
        ],
        out_specs=pl.BlockSpec((1, blk, MLA_V), lambda b, h, i, j: (b, i, h)),
        out_shape=jax.ShapeDtypeStruct((bsz, seq, nh * MLA_V), BF16),
        scratch_shapes=[
            pltpu.VMEM((blk, 1), F32),
            pltpu.VMEM((blk, 1), F32),
            pltpu.VMEM((blk, MLA_V), F32),
        ],
        compiler_params=_cparams(("parallel", "parallel", "parallel", "arbitrary")),
        name="mla_flash",
    )(q3, kn3, kr3, v3)


def _ffn_kernel(x_ref, g_ref, sc_ref, sh_ref, gt_ref, wg_ref, wu_ref, wd_ref, fg_ref, o_ref,
                h_ref, acc_ref, *, final):
    f = pl.program_id(1)

    @pl.when(f == 0)
    def _():
        h_ref[...] = _norm_mod(x_ref[...], g_ref[...], sc_ref[0], sh_ref[0]).astype(BF16)
        acc_ref[...] = jnp.zeros(acc_ref.shape, F32)

    hb = h_ref[...]
    act = (_silu(_dot(hb, wg_ref[...])) * _dot(hb, wu_ref[...])).astype(BF16)
    acc_ref[...] += _dot(act, wd_ref[...])

    @pl.when(f == pl.num_programs(1) - 1)
    def _():
        xn = x_ref[...] + gt_ref[0] * acc_ref[...]
        if final:
            xn = (xn * lax.rsqrt(jnp.mean(xn * xn, axis=-1, keepdims=True) + EPS)) * fg_ref[...]
        o_ref[...] = xn


def _ffn(x2, g, mod, wg, wu, wd, fg, seq, final):
    n, d = x2.shape
    dff = wg.shape[1]
    tm, tf = ROW_TILE, dff // 2
    per_b = seq // tm
    mod_spec = lambda k: pl.BlockSpec((1, 1, d), lambda i, f, k=k: ((i // per_b) * N_MOD + k, 0, 0))
    return pl.pallas_call(
        functools.partial(_ffn_kernel, final=final),
        grid=(n // tm, dff // tf),
        in_specs=[
            pl.BlockSpec((tm, d), lambda i, f: (i, 0)),
            pl.BlockSpec((1, d), lambda i, f: (0, 0)),
            mod_spec(4), mod_spec(3), mod_spec(5),
            pl.BlockSpec((d, tf), lambda i, f: (0, f)),
            pl.BlockSpec((d, tf), lambda i, f: (0, f)),
            pl.BlockSpec((tf, d), lambda i, f: (f, 0)),
            pl.BlockSpec((1, d), lambda i, f: (0, 0)),
        ],
        out_specs=pl.BlockSpec((tm, d), lambda i, f: (i, 0)),
        out_shape=jax.ShapeDtypeStruct((n, d), F32),
        scratch_shapes=[pltpu.VMEM((tm, d), BF16), pltpu.VMEM((tm, d), F32)],
        compiler_params=_cparams(("parallel", "arbitrary")),
        name="ffn_final" if final else "ffn",
    )(x2, g, mod, mod, mod, wg, wu, wd, fg)


def _gdn_weights(w_in, conv_w):
    main_cols = 2 * GDN_HEADS * GDN_HEAD_DIM + 2 * GDN_HEADS * GDN_HEAD_DIM
    w_main = w_in[:, :main_cols].astype(BF16)
    w_ab = jnp.pad(w_in[:, main_cols:], ((0, 0), (0, 128 - 2 * GDN_HEADS))).astype(BF16)
    cw = conv_w.T.reshape(GDN_CONV, 3, GDN_HEADS, GDN_HEAD_DIM).transpose(2, 1, 0, 3)
    return w_main, w_ab, cw


def _mla_weights(w_in, w_uq, w_ukv):
    qr, kvr, half = MLA_Q_RANK, MLA_KV_RANK, MLA_ROPE // 2
    d = w_in.shape[0]
    rope = w_in[:, qr + kvr:]
    z = jnp.zeros((d, 128 - MLA_ROPE), w_in.dtype)
    w_in_ext = jnp.concatenate(
        [w_in[:, :qr + kvr], rope, z, rope[:, half:], rope[:, :half], z], axis=1).astype(BF16)
    uq = w_uq.reshape(qr, MLA_HEADS, MLA_NOPE + MLA_ROPE)
    nope, r = uq[..., :MLA_NOPE], uq[..., MLA_NOPE:]
    zq = jnp.zeros((qr, MLA_HEADS, 128 - MLA_ROPE), w_uq.dtype)
    wq = jnp.concatenate([nope, r, zq], axis=-1).reshape(qr, MLA_HEADS * MLA_QK_PAD).astype(BF16)
    wqs = jnp.concatenate([r[..., half:], r[..., :half], zq], axis=-1).reshape(
        qr, MLA_HEADS * 128).astype(BF16)
    ukv = w_ukv.reshape(kvr, MLA_HEADS, MLA_NOPE + MLA_V)
    wkn = ukv[..., :MLA_NOPE].reshape(kvr, MLA_HEADS * MLA_NOPE).astype(BF16)
    wv = ukv[..., MLA_NOPE:].reshape(kvr, MLA_HEADS * MLA_V).astype(BF16)
    return w_in_ext, wq, wqs, wkn, wv


def _rope_table():
    half = MLA_ROPE // 2
    inv_freq = ROPE_THETA ** (-jnp.arange(0, MLA_ROPE, 2, dtype=F32) / MLA_ROPE)
    z = jnp.zeros((128 - MLA_ROPE,), F32)
    ones = jnp.ones((half,), F32)
    freq = jnp.concatenate([inv_freq, inv_freq, z])
    cmask = jnp.concatenate([ones, ones, z])
    ssign = jnp.concatenate([-ones, ones, z])
    pad = jnp.zeros((5, 128), F32)
    return jnp.concatenate([jnp.stack([freq, cmask, ssign]), pad], axis=0)


def kernel(x, c, positions, ada_w, ada_b, norm_mix_g, norm_ffn_g, gdn_w_in, gdn_conv_w, gdn_a_log,
           gdn_dt_bias, gdn_norm_g, gdn_w_out, mla_w_in, mla_q_norm_g, mla_kv_norm_g, mla_w_uq,
           mla_w_ukv, mla_w_out, ffn_w_gate, ffn_w_up, ffn_w_down, final_norm_g):
    bsz, seq, d = x.shape
    depth = ada_w.shape[0]
    n_mixers = 2
    x2 = x.reshape(bsz * seq, d)
    pos2 = positions.reshape(bsz * seq, 1)
    tab = _rope_table()

    mod_all = _adaln(c, ada_w, ada_b).reshape(depth, bsz * N_MOD, 1, d)

    for layer in range(depth):
        mod = mod_all[layer]
        j = layer // n_mixers
        g_mix = norm_mix_g[layer].reshape(1, d)
        if layer % n_mixers == 0:
            w_main, w_ab, cw = _gdn_weights(gdn_w_in[j], gdn_conv_w[j])
            main, ab = _gdn_inproj(x2, g_mix, mod, w_main, w_ab, seq)
            o = _gdn_core(main, ab, cw, gdn_a_log[j], gdn_dt_bias[j],
                          gdn_norm_g[j].reshape(1, GDN_HEAD_DIM), bsz, seq)
            x2 = _outproj_residual(o.reshape(bsz * seq, -1), gdn_w_out[j].astype(BF16), x2, mod, seq)
        else:
            w_in_ext, wq, wqs, wkn, wv = _mla_weights(mla_w_in[j], mla_w_uq[j], mla_w_ukv[j])
            qcat, kn, kr, v = _mla_proj(
                x2, g_mix, mod, pos2, tab, w_in_ext,
                mla_q_norm_g[j].reshape(1, -1), mla_kv_norm_g[j].reshape(1, -1),
                wq, wqs, wkn, wv, seq)
            o = _flash_attention(qcat, kn, kr, v, bsz, seq)
            x2 = _outproj_residual(o.reshape(bsz * seq, -1), mla_w_out[j].astype(BF16), x2, mod, seq)

        x2 = _ffn(x2, norm_ffn_g[layer].reshape(1, d), mod,
                  ffn_w_gate[layer].astype(BF16), ffn_w_up[layer].astype(BF16),
                  ffn_w_down[layer].astype(BF16), final_norm_g.reshape(1, d), seq,
                  final=(layer == depth - 1))

    return x2.reshape(bsz, seq, d)
```

```python
import functools

import jax
import jax.numpy as jnp
from jax import lax
from jax.experimental import pallas as pl
from jax.experimental.pallas import tpu as pltpu

F32 = jnp.float32
BF16 = jnp.bfloat16

D_MODEL = 1024
N_MOD = 6
EPS = 1e-6

GDN_HEADS = 8
GDN_HEAD_DIM = 128
GDN_CONV = 4
GDN_CHUNK = 64
GDN_INPROJ_COLS = 1024
GDN_INV_LEAF = 8
GDN_HEADS_PER_STEP = 4
GDN_GROUP = 16
CONV_SLAB = 256
CONV_HALO = 16

MLA_HEADS = 8
MLA_NOPE = 128
MLA_ROPE = 64
MLA_V = 128
MLA_Q_RANK = 384
MLA_KV_RANK = 256
MLA_QK_PAD = 256
ROPE_THETA = 10000.0

ROW_TILE = 512
ATTN_BLOCK = 512
LOG2_E = 1.4426950408889634
VMEM_LIMIT = 56 * 1024 * 1024


def _cparams(sem, vmem=VMEM_LIMIT):
    return pltpu.CompilerParams(dimension_semantics=sem, vmem_limit_bytes=vmem)


def _sigmoid(x):
    return 1.0 / (1.0 + jnp.exp(-x))


def _silu(x):
    return x * _sigmoid(x)


def _softplus(x):
    return jnp.maximum(x, 0.0) + jnp.log(1.0 + jnp.exp(-jnp.abs(x)))


def _dot(a, b):
    return jnp.dot(a, b, preferred_element_type=F32)


def _dot_nt(a, b):
    return lax.dot_general(a, b, (((1,), (1,)), ((), ())), preferred_element_type=F32)


def _split(a):
    hi = a.astype(BF16)
    lo = (a - hi.astype(F32)).astype(BF16)
    return hi, lo


class _Split:
    def __init__(self, a, exact=False):
        self.f32 = a
        if exact:
            self.hi, self.lo = a.astype(BF16), None
        else:
            self.hi, self.lo = _split(a)


def _dot3(a, b):
    out = _dot(a.hi, b.hi)
    if b.lo is not None:
        out = out + _dot(a.hi, b.lo)
    if a.lo is not None:
        out = out + _dot(a.lo, b.hi)
    return out


def _norm_mod(x, g, scale, shift):
    r = lax.rsqrt(jnp.mean(x * x, axis=-1, keepdims=True) + EPS)
    return (x * r) * g * (1.0 + scale) + shift


def _adaln_kernel(c_ref, w_ref, b_ref, o_ref):
    c_act = _silu(c_ref[...]).astype(BF16)
    o_ref[0] = _dot(c_act, w_ref[0].astype(BF16)) + b_ref[0]


def _adaln(c, ada_w, ada_b):
    depth, d, n = ada_w.shape
    bsz = c.shape[0]
    tn = 1536
    return pl.pallas_call(
        _adaln_kernel,
        grid=(depth, n // tn),
        in_specs=[
            pl.BlockSpec((bsz, d), lambda l, j: (0, 0)),
            pl.BlockSpec((1, d, tn), lambda l, j: (l, 0, j)),
            pl.BlockSpec((1, 1, tn), lambda l, j: (l, 0, j)),
        ],
        out_specs=pl.BlockSpec((1, bsz, tn), lambda l, j: (l, 0, j)),
        out_shape=jax.ShapeDtypeStruct((depth, bsz, n), F32),
        compiler_params=_cparams(("parallel", "parallel")),
        name="adaln_mod",
    )(c, ada_w, ada_b.reshape(depth, 1, n))


def _gdn_inproj_kernel(x_ref, g_ref, sc_ref, sh_ref, w_ref, wab_ref, o_ref, ab_ref):
    h = _norm_mod(x_ref[...], g_ref[...], sc_ref[0], sh_ref[0]).astype(BF16)
    ab_ref[...] = _dot(h, wab_ref[...])
    nout = o_ref.shape[1]
    for c0 in range(0, nout, GDN_INPROJ_COLS):
        cols = slice(c0, c0 + GDN_INPROJ_COLS)
        o_ref[:, cols] = _dot(h, w_ref[:, cols]).astype(o_ref.dtype)


def _gdn_inproj(x2, g, mod, w_main, w_ab, seq):
    n, d = x2.shape
    nout = w_main.shape[1]
    tm = ROW_TILE
    per_b = seq // tm
    return pl.pallas_call(
        _gdn_inproj_kernel,
        grid=(n // tm,),
        in_specs=[
            pl.BlockSpec((tm, d), lambda i: (i, 0)),
            pl.BlockSpec((1, d), lambda i: (0, 0)),
            pl.BlockSpec((1, 1, d), lambda i: ((i // per_b) * N_MOD + 1, 0, 0)),
            pl.BlockSpec((1, 1, d), lambda i: ((i // per_b) * N_MOD + 0, 0, 0)),
            pl.BlockSpec((d, nout), lambda i: (0, 0)),
            pl.BlockSpec((d, 128), lambda i: (0, 0)),
        ],
        out_specs=[
            pl.BlockSpec((tm, nout), lambda i: (i, 0)),
            pl.BlockSpec((tm, 128), lambda i: (i, 0)),
        ],
        out_shape=[
            jax.ShapeDtypeStruct((n, nout), BF16),
            jax.ShapeDtypeStruct((n, 128), F32),
        ],
        compiler_params=_cparams(("parallel",)),
        name="gdn_inproj",
    )(x2, g, mod, mod, w_main, w_ab)


def _run_interleaved(gens):
    while gens:
        alive = []
        for gen in gens:
            try:
                next(gen)
                alive.append(gen)
            except StopIteration:
                pass
        gens = alive


def _gdn_core_kernel(alog_ref, dtb_ref, q_ref, k_ref, v_ref, gate_ref, ab_ref, cw_ref, ng_ref,
                     o_ref, xc_ref, p_ref, ou_ref, m_ref, r_ref, egl_ref):
    hps = GDN_HEADS_PER_STEP
    hd = GDN_HEAD_DIM
    head0 = pl.program_id(1) * hps
    seq = q_ref.shape[1]
    cl = GDN_CHUNK
    n_chunks = seq // cl

    row = lax.broadcasted_iota(jnp.int32, (cl, cl), 0)
    col = lax.broadcasted_iota(jnp.int32, (cl, cl), 1)
    tril = row >= col
    strict = row > col
    tril_b = tril.astype(F32).astype(BF16)
    eye = (row == col).astype(F32)
    lane = lax.broadcasted_iota(jnp.int32, (cl, 128), 1)
    def same_block(size):
        shift = size.bit_length() - 1
        return jnp.right_shift(row, shift) == jnp.right_shift(col, shift)

    leaf_mask = same_block(GDN_INV_LEAF)
    merge_masks = {}
    size = GDN_INV_LEAF
    while size < cl:
        merge_masks[size] = same_block(2 * size) & jnp.logical_not(same_block(size))
        size *= 2

    for hh in range(hps):
        head = head0 + hh
        lanes = slice(hh * hd, (hh + 1) * hd)

        def conv_slab(src_ref, which, s, first, hh=hh, lanes=lanes):
            if first:
                body = src_ref[0, 0:CONV_SLAB, lanes].astype(F32)
                ext = jnp.concatenate([jnp.zeros((CONV_HALO, hd), F32), body], axis=0)
                r0 = 0
            else:
                start = pl.multiple_of(s * CONV_SLAB - CONV_HALO, CONV_HALO)
                ext = src_ref[0, pl.ds(start, CONV_SLAB + CONV_HALO), lanes].astype(F32)
                r0 = pl.multiple_of(s * CONV_SLAB, CONV_SLAB)
            w = cw_ref[hh, which]
            acc = ext[CONV_HALO:] * w[GDN_CONV - 1:GDN_CONV]
            for sh in range(1, GDN_CONV):
                tap = GDN_CONV - 1 - sh
                acc = acc + pltpu.roll(ext, sh, 0)[CONV_HALO:] * w[tap:tap + 1]
            y = _silu(acc)
            if which < 2:
                y = y * lax.rsqrt(jnp.sum(y * y, axis=-1, keepdims=True) + EPS)
            if which == 0:
                y = y * (hd ** -0.5)
            xc_ref[which, pl.ds(r0, CONV_SLAB), :] = y

        for which, src in enumerate((q_ref, k_ref, v_ref)):
            conv_slab(src, which, 0, True)

            def slab_body(s, carry, src=src, which=which, conv_slab=conv_slab):
                conv_slab(src, which, s, False)
                return carry

            lax.fori_loop(1, seq // CONV_SLAB, slab_body, 0)

        a_head = jnp.exp(jnp.full((1, 1), alog_ref[head], F32))
        dt_head = dtb_ref[head]

        def prep_chunk(c, hh=hh, head=head, a_head=a_head, dt_head=dt_head):
            r0 = pl.multiple_of(c * cl, cl)
            q = xc_ref[0, pl.ds(r0, cl), :]
            k = xc_ref[1, pl.ds(r0, cl), :]
            v = xc_ref[2, pl.ds(r0, cl), :]
            ab = ab_ref[0, pl.ds(r0, cl), :]
            yield
            a_col = jnp.sum(jnp.where(lane == head, ab, 0.0), axis=1, keepdims=True)
            b_col = jnp.sum(jnp.where(lane == head + GDN_HEADS, ab, 0.0), axis=1, keepdims=True)
            g = -a_head * _softplus(a_col + dt_head)
            beta = _sigmoid(b_col)
            g_hi, g_lo = _split(jnp.broadcast_to(g, (cl, 128)))
            gc = _dot(tril_b, g_hi) + _dot(tril_b, g_lo)
            kb = k * beta
            k_b16 = k.astype(BF16)
            kk = _dot_nt(kb.astype(BF16), k_b16)
            qk = _dot_nt(q.astype(BF16), k_b16)
            yield
            gl = gc[cl - 1:cl, :]
            e_gc = jnp.exp(gc)
            e_rev = jnp.exp(gl - gc)
            e_gl = jnp.exp(gl)
            diff = gc[:, :cl] - gc.T[:cl, :]
            dec = jnp.exp(jnp.where(tril, diff, -jnp.inf))
            lmat = jnp.where(strict, kk * dec, 0.0).astype(BF16).astype(F32)
            attn = (qk * dec).astype(BF16)
            rhs = jnp.concatenate([kb * e_gc, v * beta], axis=1).astype(BF16)
            kdt = (k * e_rev).T.astype(BF16)
            qd = q * e_gc
            nb = _Split(-jnp.where(leaf_mask, lmat, 0.0), exact=True)
            tinv = eye + nb.f32
            tinv_exact = True
            pw = nb
            span = 2
            while span < GDN_INV_LEAF:
                pw = _Split(_dot3(pw, pw))
                yield
                tinv = tinv + _dot3(_Split(tinv, exact=tinv_exact), pw)
                tinv_exact = False
                span *= 2
            size = GDN_INV_LEAF
            while size < cl:
                yield
                ts = _Split(tinv)
                xm = _dot3(_Split(jnp.where(merge_masks[size], lmat, 0.0), exact=True), ts)
                yield
                tinv = tinv - _dot3(ts, _Split(xm))
                size *= 2
            yield
            ts = _Split(tinv)
            wu = (_dot(ts.hi, rhs) + _dot(ts.lo, rhs)).astype(BF16)
            yield
            aw = _dot(attn, wu)
            kw = _dot(kdt, wu)
            yield
            p_ref[hh, c] = (qd - aw[:, :hd]).astype(BF16)
            ou_ref[hh, c] = aw[:, hd:]
            m_ref[hh, c] = (-kw[:, :hd]).astype(BF16)
            r_ref[hh, c] = kw[:, hd:]
            egl_ref[hh, c] = jnp.broadcast_to(e_gl, (8, 128))

        def prep_body(i, carry, prep_chunk=prep_chunk):
            _run_interleaved([prep_chunk(i * GDN_GROUP + u) for u in range(GDN_GROUP)])
            return carry

        lax.fori_loop(0, n_chunks // GDN_GROUP, prep_body, 0)

    ng = ng_ref[...]

    def scan_body(c, states):
        r0 = pl.multiple_of(c * cl, cl)
        new_states = [None] * hps

        def head_step(hh):
            s = states[hh]
            p = p_ref[hh, c]
            m = m_ref[hh, c]
            ou = ou_ref[hh, c]
            r = r_ref[hh, c]
            eg = egl_ref[hh, c][0:1, :]
            gt = gate_ref[0, pl.ds(r0, cl), hh * hd:(hh + 1) * hd].astype(F32)
            yield
            sb = s.astype(BF16)
            o = _dot(p, sb) + ou
            new_states[hh] = eg * s + (_dot(m, sb) + r)
            rn = lax.rsqrt(jnp.mean(o * o, axis=-1, keepdims=True) + EPS)
            y = ((o * rn) * ng * _silu(gt)).astype(o_ref.dtype)
            yield
            o_ref[0, pl.ds(r0, cl), hh * hd:(hh + 1) * hd] = y

        _run_interleaved([head_step(hh) for hh in range(hps)])
        return tuple(new_states)

    init = tuple(jnp.zeros((hd, hd), F32) for _ in range(hps))
    lax.fori_loop(0, n_chunks, scan_body, init)


def _gdn_core(main, ab, cw, a_log, dt_bias, norm_g, bsz, seq):
    hd = GDN_HEAD_DIM
    nh = GDN_HEADS
    hps = GDN_HEADS_PER_STEP
    groups = nh // hps
    n_chunks = seq // GDN_CHUNK
    main3 = main.reshape(bsz, seq, main.shape[1])
    ab3 = ab.reshape(bsz, seq, 128)
    smem = pl.BlockSpec(memory_space=pltpu.SMEM)

    def col_spec(which):
        return pl.BlockSpec((1, seq, hps * hd), lambda b, h, which=which: (b, 0, which * groups + h))

    return pl.pallas_call(
        _gdn_core_kernel,
        grid=(bsz, groups),
        in_specs=[
            smem, smem,
            col_spec(0), col_spec(1), col_spec(2), col_spec(3),
            pl.BlockSpec((1, seq, 128), lambda b, h: (b, 0, 0)),
            pl.BlockSpec((hps, 3, GDN_CONV, hd), lambda b, h: (h, 0, 0, 0)),
            pl.BlockSpec((1, hd), lambda b, h: (0, 0)),
        ],
        out_specs=pl.BlockSpec((1, seq, hps * hd), lambda b, h: (b, 0, h)),
        out_shape=jax.ShapeDtypeStruct((bsz, seq, nh * hd), BF16),
        scratch_shapes=[
            pltpu.VMEM((3, seq, hd), F32),
            pltpu.VMEM((hps, n_chunks, GDN_CHUNK, hd), BF16),
            pltpu.VMEM((hps, n_chunks, GDN_CHUNK, hd), F32),
            pltpu.VMEM((hps, n_chunks, hd, hd), BF16),
            pltpu.VMEM((hps, n_chunks, hd, hd), F32),
            pltpu.VMEM((hps, n_chunks, 8, 128), F32),
        ],
        compiler_params=_cparams(("parallel", "parallel")),
        name="gdn_core",
    )(a_log, dt_bias, main3, main3, main3, main3, ab3, cw, norm_g)


def _outproj_kernel(a_ref, w_ref, x_ref, gt_ref, o_ref):
    o_ref[...] = x_ref[...] + gt_ref[0] * _dot(a_ref[...], w_ref[...])


def _outproj_residual(a2, w, x2, mod, seq):
    n, d = x2.shape
    k = a2.shape[1]
    tm = ROW_TILE
    per_b = seq // tm
    return pl.pallas_call(
        _outproj_kernel,
        grid=(n // tm,),
        in_specs=[
            pl.BlockSpec((tm, k), lambda i: (i, 0)),
            pl.BlockSpec((k, d), lambda i: (0, 0)),
            pl.BlockSpec((tm, d), lambda i: (i, 0)),
            pl.BlockSpec((1, 1, d), lambda i: ((i // per_b) * N_MOD + 2, 0, 0)),
        ],
        out_specs=pl.BlockSpec((tm, d), lambda i: (i, 0)),
        out_shape=jax.ShapeDtypeStruct((n, d), F32),
        compiler_params=_cparams(("parallel",)),
        name="outproj_residual",
    )(a2, w, x2, mod)


def _mla_proj_kernel(x_ref, g_ref, sc_ref, sh_ref, pos_ref, tab_ref, win_ref, qg_ref, kvg_ref,
                     wq_ref, wqs_ref, wkn_ref, wv_ref, q_ref, kn_ref, kr_ref, v_ref):
    h = _norm_mod(x_ref[...], g_ref[...], sc_ref[0], sh_ref[0]).astype(BF16)
    proj = _dot(h, win_ref[...])
    qr, kvr = MLA_Q_RANK, MLA_KV_RANK
    c_q = proj[:, :qr]
    c_kv = proj[:, qr:qr + kvr]
    kr_a = proj[:, qr + kvr:qr + kvr + 128]
    kr_b = proj[:, qr + kvr + 128:qr + kvr + 256]

    def rms(t, gain):
        return (t * lax.rsqrt(jnp.mean(t * t, axis=-1, keepdims=True) + EPS)) * gain

    cqn = rms(c_q, qg_ref[...])
    ckvn = rms(c_kv, kvg_ref[...])
    cqn_t = cqn.T.astype(BF16)
    ckvn_t = ckvn.T.astype(BF16)
    ckvn = ckvn.astype(BF16)

    ang = pos_ref[...].astype(F32) * tab_ref[0:1, :]
    ct = jnp.cos(ang) * tab_ref[1:2, :]
    st = jnp.sin(ang) * tab_ref[2:3, :]
    kr_ref[...] = (kr_a * ct + kr_b * st).astype(kr_ref.dtype)
    ct_t = ct.T[:MLA_ROPE, :]
    st_t = st.T[:MLA_ROPE, :]

    scale = (MLA_NOPE + MLA_ROPE) ** -0.5 * LOG2_E
    qf = _dot(wq_ref[...], cqn_t)
    qs = _dot(wqs_ref[...], cqn_t)
    for hh in range(MLA_HEADS):
        base = hh * MLA_QK_PAD
        rope0 = base + MLA_NOPE
        q_ref[0, base:rope0, :] = (qf[base:rope0, :] * scale).astype(q_ref.dtype)
        rot = qf[rope0:rope0 + MLA_ROPE, :] * ct_t + qs[hh * MLA_ROPE:(hh + 1) * MLA_ROPE, :] * st_t
        q_ref[0, rope0:rope0 + MLA_ROPE, :] = (rot * scale).astype(q_ref.dtype)
        q_ref[0, rope0 + MLA_ROPE:base + MLA_QK_PAD, :] = jnp.zeros(
            (MLA_QK_PAD - MLA_NOPE - MLA_ROPE, q_ref.shape[2]), q_ref.dtype)

    kn_ref[...] = _dot(ckvn, wkn_ref[...]).astype(kn_ref.dtype)
    v_ref[0] = _dot(wv_ref[...], ckvn_t).astype(v_ref.dtype)


def _mla_proj(x2, g, mod, pos2, tab, w_in, qg, kvg, wq, wqs, wkn, wv, seq):
    n, d = x2.shape
    tm = ROW_TILE
    per_b = seq // tm
    full = lambda a: pl.BlockSpec(a.shape, lambda i: (0,) * a.ndim)
    hq = MLA_HEADS * MLA_QK_PAD
    hv = MLA_HEADS * MLA_V
    return pl.pallas_call(
        _mla_proj_kernel,
        grid=(n // tm,),
        in_specs=[
            pl.BlockSpec((tm, d), lambda i: (i, 0)),
            full(g),
            pl.BlockSpec((1, 1, d), lambda i: ((i // per_b) * N_MOD + 1, 0, 0)),
            pl.BlockSpec((1, 1, d), lambda i: ((i // per_b) * N_MOD + 0, 0, 0)),
            pl.BlockSpec((tm, 1), lambda i: (i, 0)),
            full(tab), full(w_in), full(qg), full(kvg), full(wq), full(wqs), full(wkn), full(wv),
        ],
        out_specs=[
            pl.BlockSpec((1, hq, tm), lambda i: (i // per_b, 0, i % per_b)),
            pl.BlockSpec((tm, hv), lambda i: (i, 0)),
            pl.BlockSpec((tm, 128), lambda i: (i, 0)),
            pl.BlockSpec((1, hv, tm), lambda i: (i // per_b, 0, i % per_b)),
        ],
        out_shape=[
            jax.ShapeDtypeStruct((n // seq, hq, seq), BF16),
            jax.ShapeDtypeStruct((n, hv), BF16),
            jax.ShapeDtypeStruct((n, 128), BF16),
            jax.ShapeDtypeStruct((n // seq, hv, seq), BF16),
        ],
        compiler_params=_cparams(("parallel",)),
        name="mla_proj",
    )(x2, g, mod, mod, pos2, tab, w_in, qg, kvg, wq, wqs, wkn, wv)


def _flash_kernel(q_ref, kn_ref, kr_ref, v_ref, o_ref):
    qi = pl.program_id(2)
    blk = q_ref.shape[2]
    n_blocks = kn_ref.shape[1] // blk

    def attend(n_keys):
        q_t = q_ref[0]
        m = jnp.full((1, blk), -jnp.inf, F32)
        l = jnp.zeros((1, blk), F32)
        acc = jnp.zeros((MLA_V, blk), F32)
        for j in range(n_keys):
            ks = slice(j * blk, (j + 1) * blk)
            kcat = jnp.concatenate([kn_ref[0, ks, :], kr_ref[0, ks, :]], axis=1)
            s = _dot(kcat, q_t)
            if j == n_keys - 1:
                key = lax.broadcasted_iota(jnp.int32, s.shape, 0)
                qry = lax.broadcasted_iota(jnp.int32, s.shape, 1)
                s = jnp.where(key <= qry, s, -jnp.inf)
            m_new = jnp.maximum(m, jnp.max(s, axis=0, keepdims=True))
            alpha = jnp.exp2(m - m_new)
            p = jnp.exp2(s - m_new)
            l = alpha * l + jnp.sum(p, axis=0, keepdims=True)
            acc = alpha * acc + _dot(v_ref[0, :, ks], p.astype(BF16))
            m = m_new
        o_ref[0] = (acc / l).T.astype(o_ref.dtype)

    for i in range(n_blocks):
        pl.when(qi == i)(functools.partial(attend, i + 1))


def _flash_attention(q_t, kn, kr, v_t, bsz, seq):
    nh = MLA_HEADS
    blk = ATTN_BLOCK
    kn3 = kn.reshape(bsz, seq, nh * MLA_NOPE)
    kr3 = kr.reshape(bsz, seq, 128)
    return pl.pallas_call(
        _flash_kernel,
        grid=(bsz, nh, seq // blk),
        in_specs=[
            pl.BlockSpec((1, MLA_QK_PAD, blk), lambda b, h, i: (b, h, i)),
            pl.BlockSpec((1, seq, MLA_NOPE), lambda b, h, i: (b, 0, h)),
            pl.BlockSpec((1, seq, 128), lambda b, h, i: (b, 0, 0)),
            pl.BlockSpec((1, MLA_V, seq), lambda b, h, i: (b, h, 0)),
        ],
        out_specs=pl.BlockSpec((1, blk, MLA_V), lambda b, h, i: (b, i, h)),
        out_shape=jax.ShapeDtypeStruct((bsz, seq, nh * MLA_V), BF16),
        compiler_params=_cparams(("parallel", "parallel", "parallel")),
        name="mla_flash",
    )(q_t, kn3, kr3, v_t)


def _ffn_kernel(x_ref, g_ref, sc_ref, sh_ref, gt_ref, wg_ref, wu_ref, wd_ref, fg_ref, o_ref,
                h_ref, acc_ref, *, final):
    f = pl.program_id(1)

    @pl.when(f == 0)
    def _():
        h_ref[...] = _norm_mod(x_ref[...], g_ref[...], sc_ref[0], sh_ref[0]).astype(BF16)
        acc_ref[...] = jnp.zeros(acc_ref.shape, F32)

    hb = h_ref[...]
    act = (_silu(_dot(hb, wg_ref[...])) * _dot(hb, wu_ref[...])).astype(BF16)
    acc_ref[...] += _dot(act, wd_ref[...])

    @pl.when(f == pl.num_programs(1) - 1)
    def _():
        xn = x_ref[...] + gt_ref[0] * acc_ref[...]
        if final:
            xn = (xn * lax.rsqrt(jnp.mean(xn * xn, axis=-1, keepdims=True) + EPS)) * fg_ref[...]
        o_ref[...] = xn


def _ffn(x2, g, mod, wg, wu, wd, fg, seq, final):
    n, d = x2.shape
    dff = wg.shape[1]
    tm, tf = ROW_TILE, dff // 2
    per_b = seq // tm
    mod_spec = lambda k: pl.BlockSpec((1, 1, d), lambda i, f, k=k: ((i // per_b) * N_MOD + k, 0, 0))
    return pl.pallas_call(
        functools.partial(_ffn_kernel, final=final),
        grid=(n // tm, dff // tf),
        in_specs=[
            pl.BlockSpec((tm, d), lambda i, f: (i, 0)),
            pl.BlockSpec((1, d), lambda i, f: (0, 0)),
            mod_spec(4), mod_spec(3), mod_spec(5),
            pl.BlockSpec((d, tf), lambda i, f: (0, f)),
            pl.BlockSpec((d, tf), lambda i, f: (0, f)),
            pl.BlockSpec((tf, d), lambda i, f: (f, 0)),
            pl.BlockSpec((1, d), lambda i, f: (0, 0)),
        ],
        out_specs=pl.BlockSpec((tm, d), lambda i, f: (i, 0)),
        out_shape=jax.ShapeDtypeStruct((n, d), F32),
        scratch_shapes=[pltpu.VMEM((tm, d), BF16), pltpu.VMEM((tm, d), F32)],
        compiler_params=_cparams(("parallel", "arbitrary")),
        name="ffn_final" if final else "ffn",
    )(x2, g, mod, mod, mod, wg, wu, wd, fg)


def _gdn_weights(w_in, conv_w):
    main_cols = 2 * GDN_HEADS * GDN_HEAD_DIM + 2 * GDN_HEADS * GDN_HEAD_DIM
    w_main = w_in[:, :main_cols].astype(BF16)
    w_ab = jnp.pad(w_in[:, main_cols:], ((0, 0), (0, 128 - 2 * GDN_HEADS))).astype(BF16)
    cw = conv_w.T.reshape(GDN_CONV, 3, GDN_HEADS, GDN_HEAD_DIM).transpose(2, 1, 0, 3)
    return w_main, w_ab, cw


def _mla_weights(w_in, w_uq, w_ukv):
    qr, kvr, half = MLA_Q_RANK, MLA_KV_RANK, MLA_ROPE // 2
    d = w_in.shape[0]
    rope = w_in[:, qr + kvr:]
    z = jnp.zeros((d, 128 - MLA_ROPE), w_in.dtype)
    w_in_ext = jnp.concatenate(
        [w_in[:, :qr + kvr], rope, z, rope[:, half:], rope[:, :half], z], axis=1).astype(BF16)
    uq = w_uq.reshape(qr, MLA_HEADS, MLA_NOPE + MLA_ROPE)
    nope, r = uq[..., :MLA_NOPE], uq[..., MLA_NOPE:]
    zq = jnp.zeros((qr, MLA_HEADS, 128 - MLA_ROPE), w_uq.dtype)
    wq = jnp.concatenate([nope, r, zq], axis=-1).reshape(qr, MLA_HEADS * MLA_QK_PAD).T.astype(BF16)
    wqs = jnp.concatenate([r[..., half:], r[..., :half]], axis=-1).reshape(
        qr, MLA_HEADS * MLA_ROPE).T.astype(BF16)
    ukv = w_ukv.reshape(kvr, MLA_HEADS, MLA_NOPE + MLA_V)
    wkn = ukv[..., :MLA_NOPE].reshape(kvr, MLA_HEADS * MLA_NOPE).astype(BF16)
    wv = ukv[..., MLA_NOPE:].reshape(kvr, MLA_HEADS * MLA_V).T.astype(BF16)
    return w_in_ext, wq, wqs, wkn, wv


def _rope_table():
    half = MLA_ROPE // 2
    inv_freq = ROPE_THETA ** (-jnp.arange(0, MLA_ROPE, 2, dtype=F32) / MLA_ROPE)
    z = jnp.zeros((128 - MLA_ROPE,), F32)
    ones = jnp.ones((half,), F32)
    freq = jnp.concatenate([inv_freq, inv_freq, z])
    cmask = jnp.concatenate([ones, ones, z])
    ssign = jnp.concatenate([-ones, ones, z])
    pad = jnp.zeros((5, 128), F32)
    return jnp.concatenate([jnp.stack([freq, cmask, ssign]), pad], axis=0)


def kernel(x, c, positions, ada_w, ada_b, norm_mix_g, norm_ffn_g, gdn_w_in, gdn_conv_w, gdn_a_log,
           gdn_dt_bias, gdn_norm_g, gdn_w_out, mla_w_in, mla_q_norm_g, mla_kv_norm_g, mla_w_uq,
           mla_w_ukv, mla_w_out, ffn_w_gate, ffn_w_up, ffn_w_down, final_norm_g):
    bsz, seq, d = x.shape
    depth = ada_w.shape[0]
    n_mixers = 2
    x2 = x.reshape(bsz * seq, d)
    pos2 = positions.reshape(bsz * seq, 1)
    tab = _rope_table()

    mod_all = _adaln(c, ada_w, ada_b).reshape(depth, bsz * N_MOD, 1, d)

    for layer in range(depth):
        mod = mod_all[layer]
        j = layer // n_mixers
        g_mix = norm_mix_g[layer].reshape(1, d)
        if layer % n_mixers == 0:
            w_main, w_ab, cw = _gdn_weights(gdn_w_in[j], gdn_conv_w[j])
            main, ab = _gdn_inproj(x2, g_mix, mod, w_main, w_ab, seq)
            o = _gdn_core(main, ab, cw, gdn_a_log[j], gdn_dt_bias[j],
                          gdn_norm_g[j].reshape(1, GDN_HEAD_DIM), bsz, seq)
            x2 = _outproj_residual(o.reshape(bsz * seq, -1), gdn_w_out[j].astype(BF16), x2, mod, seq)
        else:
            w_in_ext, wq, wqs, wkn, wv = _mla_weights(mla_w_in[j], mla_w_uq[j], mla_w_ukv[j])
            qcat, kn, kr, v = _mla_proj(
                x2, g_mix, mod, pos2, tab, w_in_ext,
                mla_q_norm_g[j].reshape(1, -1), mla_kv_norm_g[j].reshape(1, -1),
                wq, wqs, wkn, wv, seq)
            o = _flash_attention(qcat, kn, kr, v, bsz, seq)
            x2 = _outproj_residual(o.reshape(bsz * seq, -1), mla_w_out[j].astype(BF16), x2, mod, seq)

        x2 = _ffn(x2, norm_ffn_g[layer].reshape(1, d), mod,
                  ffn_w_gate[layer].astype(BF16), ffn_w_up[layer].astype(BF16),
                  ffn_w_down[layer].astype(BF16), final_norm_g.reshape(1, d), seq,
                  final=(layer == depth - 1))

    return x2.reshape(bsz, seq, d)
```

```python
import functools

import jax
import jax.numpy as jnp
from jax import lax
from jax.experimental import pallas as pl
from jax.experimental.pallas import tpu as pltpu

F32 = jnp.float32
BF16 = jnp.bfloat16

D_MODEL = 1024
N_MOD = 6
EPS = 1e-6

GDN_HEADS = 8
GDN_HEAD_DIM = 128
GDN_CONV = 4
GDN_CHUNK = 64
GDN_INPROJ_COLS = 1024
GDN_INV_LEAF = 8
GDN_HEADS_PER_STEP = 4
GDN_GROUP = 16
CONV_SLAB = 256
CONV_HALO = 16

MLA_HEADS = 8
MLA_NOPE = 128
MLA_ROPE = 64
MLA_V = 128
MLA_Q_RANK = 384
MLA_KV_RANK = 256
MLA_QK_PAD = 256
ROPE_THETA = 10000.0

ROW_TILE = 512
FFN_SPLIT = 2
ATTN_BLOCK = 512
LOG2_E = 1.4426950408889634
VMEM_LIMIT = 56 * 1024 * 1024


def _cparams(sem, vmem=VMEM_LIMIT):
    return pltpu.CompilerParams(dimension_semantics=sem, vmem_limit_bytes=vmem)


def _sigmoid(x):
    return 1.0 / (1.0 + jnp.exp(-x))


def _silu(x):
    return x * _sigmoid(x)


def _softplus(x):
    return jnp.maximum(x, 0.0) + jnp.log(1.0 + jnp.exp(-jnp.abs(x)))


def _dot(a, b):
    return jnp.dot(a, b, preferred_element_type=F32)


def _dot_nt(a, b):
    return lax.dot_general(a, b, (((1,), (1,)), ((), ())), preferred_element_type=F32)


def _split(a):
    hi = a.astype(BF16)
    lo = (a - hi.astype(F32)).astype(BF16)
    return hi, lo


class _Split:
    def __init__(self, a, exact=False):
        self.f32 = a
        if exact:
            self.hi, self.lo = a.astype(BF16), None
        else:
            self.hi, self.lo = _split(a)


def _dot3(a, b):
    out = _dot(a.hi, b.hi)
    if b.lo is not None:
        out = out + _dot(a.hi, b.lo)
    if a.lo is not None:
        out = out + _dot(a.lo, b.hi)
    return out


def _norm_mod(x, g, scale, shift):
    r = lax.rsqrt(jnp.mean(x * x, axis=-1, keepdims=True) + EPS)
    return (x * r) * g * (1.0 + scale) + shift


def _adaln_kernel(c_ref, w_ref, b_ref, o_ref):
    c_act = _silu(c_ref[...]).astype(BF16)
    o_ref[0] = _dot(c_act, w_ref[0].astype(BF16)) + b_ref[0]


def _adaln(c, ada_w, ada_b):
    depth, d, n = ada_w.shape
    bsz = c.shape[0]
    tn = 1536
    return pl.pallas_call(
        _adaln_kernel,
        grid=(depth, n // tn),
        in_specs=[
            pl.BlockSpec((bsz, d), lambda l, j: (0, 0)),
            pl.BlockSpec((1, d, tn), lambda l, j: (l, 0, j)),
            pl.BlockSpec((1, 1, tn), lambda l, j: (l, 0, j)),
        ],
        out_specs=pl.BlockSpec((1, bsz, tn), lambda l, j: (l, 0, j)),
        out_shape=jax.ShapeDtypeStruct((depth, bsz, n), F32),
        compiler_params=_cparams(("parallel", "parallel")),
        name="adaln_mod",
    )(c, ada_w, ada_b.reshape(depth, 1, n))


def _gdn_inproj_kernel(x_ref, g_ref, sc_ref, sh_ref, w_ref, wab_ref, o_ref, ab_ref):
    h = _norm_mod(x_ref[...], g_ref[...], sc_ref[0], sh_ref[0]).astype(BF16)
    ab_ref[...] = _dot(h, wab_ref[...])
    nout = o_ref.shape[1]
    for c0 in range(0, nout, GDN_INPROJ_COLS):
        cols = slice(c0, c0 + GDN_INPROJ_COLS)
        o_ref[:, cols] = _dot(h, w_ref[:, cols]).astype(o_ref.dtype)


def _gdn_inproj(x2, g, mod, w_main, w_ab, seq):
    n, d = x2.shape
    nout = w_main.shape[1]
    tm = ROW_TILE
    per_b = seq // tm
    return pl.pallas_call(
        _gdn_inproj_kernel,
        grid=(n // tm,),
        in_specs=[
            pl.BlockSpec((tm, d), lambda i: (i, 0)),
            pl.BlockSpec((1, d), lambda i: (0, 0)),
            pl.BlockSpec((1, 1, d), lambda i: ((i // per_b) * N_MOD + 1, 0, 0)),
            pl.BlockSpec((1, 1, d), lambda i: ((i // per_b) * N_MOD + 0, 0, 0)),
            pl.BlockSpec((d, nout), lambda i: (0, 0)),
            pl.BlockSpec((d, 128), lambda i: (0, 0)),
        ],
        out_specs=[
            pl.BlockSpec((tm, nout), lambda i: (i, 0)),
            pl.BlockSpec((tm, 128), lambda i: (i, 0)),
        ],
        out_shape=[
            jax.ShapeDtypeStruct((n, nout), BF16),
            jax.ShapeDtypeStruct((n, 128), F32),
        ],
        compiler_params=_cparams(("parallel",)),
        name="gdn_inproj",
    )(x2, g, mod, mod, w_main, w_ab)


def _run_interleaved(gens):
    while gens:
        alive = []
        for gen in gens:
            try:
                next(gen)
                alive.append(gen)
            except StopIteration:
                pass
        gens = alive


def _gdn_core_kernel(alog_ref, dtb_ref, q_ref, k_ref, v_ref, gate_ref, ab_ref, cw_ref, ng_ref,
                     o_ref, xc_ref, p_ref, ou_ref, m_ref, r_ref, egl_ref, tril_ref):
    hps = GDN_HEADS_PER_STEP
    hd = GDN_HEAD_DIM
    head0 = pl.program_id(1) * hps
    seq = q_ref.shape[1]
    cl = GDN_CHUNK
    n_chunks = seq // cl

    row = lax.broadcasted_iota(jnp.int32, (cl, cl), 0)
    col = lax.broadcasted_iota(jnp.int32, (cl, cl), 1)
    tril = row >= col
    strict = row > col
    tril_ref[...] = tril.astype(F32).astype(BF16)
    eye = (row == col).astype(F32)
    lane = lax.broadcasted_iota(jnp.int32, (cl, 128), 1)
    def same_block(size):
        shift = size.bit_length() - 1
        return jnp.right_shift(row, shift) == jnp.right_shift(col, shift)

    leaf_mask = same_block(GDN_INV_LEAF)
    merge_masks = {}
    size = GDN_INV_LEAF
    while size < cl:
        merge_masks[size] = same_block(2 * size) & jnp.logical_not(same_block(size))
        size *= 2

    conv_defs, prep_defs = [], []
    for hh in range(hps):
        head = head0 + hh
        lanes = slice(hh * hd, (hh + 1) * hd)

        def conv_slab(which, r0, delay, hh=hh, lanes=lanes):
            src_ref = (q_ref, k_ref, v_ref)[which]
            body = src_ref[0, pl.ds(r0, CONV_SLAB), lanes].astype(F32)
            h0 = pl.multiple_of(jnp.maximum(r0 - CONV_HALO, 0), CONV_HALO)
            halo = src_ref[0, pl.ds(h0, CONV_HALO), lanes].astype(F32)
            halo = jnp.where(r0 > 0, halo, 0.0)
            ext = jnp.concatenate([halo, body], axis=0)
            w = cw_ref[hh, which]
            for _ in range(1 + delay):
                yield
            acc = ext[CONV_HALO:] * w[GDN_CONV - 1:GDN_CONV]
            for sh in range(1, GDN_CONV):
                tap = GDN_CONV - 1 - sh
                acc = acc + pltpu.roll(ext, sh, 0)[CONV_HALO:] * w[tap:tap + 1]
            y = _silu(acc)
            if which < 2:
                y = y * lax.rsqrt(jnp.sum(y * y, axis=-1, keepdims=True) + EPS)
            if which == 0:
                y = y * (hd ** -0.5)
            yield
            xc_ref[hh % 2, which, pl.ds(r0, CONV_SLAB), :] = y

        conv_defs.append(conv_slab)

        a_head = jnp.exp(jnp.full((1, 1), alog_ref[head], F32))
        dt_head = dtb_ref[head]

        def prep_chunk(c, hh=hh, head=head, a_head=a_head, dt_head=dt_head):
            r0 = pl.multiple_of(c * cl, cl)
            q = xc_ref[hh % 2, 0, pl.ds(r0, cl), :]
            k = xc_ref[hh % 2, 1, pl.ds(r0, cl), :]
            v = xc_ref[hh % 2, 2, pl.ds(r0, cl), :]
            ab = ab_ref[0, pl.ds(r0, cl), :]
            yield
            a_col = jnp.sum(jnp.where(lane == head, ab, 0.0), axis=1, keepdims=True)
            b_col = jnp.sum(jnp.where(lane == head + GDN_HEADS, ab, 0.0), axis=1, keepdims=True)
            g = -a_head * _softplus(a_col + dt_head)
            beta = _sigmoid(b_col)
            g_hi, g_lo = _split(jnp.broadcast_to(g, (cl, 128)))
            tril_b = tril_ref[...]
            gc = _dot(tril_b, g_hi) + _dot(tril_b, g_lo)
            kb = k * beta
            k_b16 = k.astype(BF16)
            kk = _dot_nt(kb.astype(BF16), k_b16)
            qk = _dot_nt(q.astype(BF16), k_b16)
            yield
            gl = gc[cl - 1:cl, :]
            e_gc = jnp.exp(gc)
            e_rev = jnp.exp(gl - gc)
            e_gl = jnp.exp(gl)
            diff = gc[:, :cl] - gc[:, :cl].T
            dec = jnp.exp(jnp.where(tril, diff, -jnp.inf))
            lmat = jnp.where(strict, kk * dec, 0.0).astype(BF16).astype(F32)
            attn = (qk * dec).astype(BF16)
            rhs = jnp.concatenate([kb * e_gc, v * beta], axis=1).astype(BF16)
            kdt = (k * e_rev).T.astype(BF16)
            qd = q * e_gc
            nb = _Split(-jnp.where(leaf_mask, lmat, 0.0), exact=True)
            tinv = eye + nb.f32
            tinv_exact = True
            pw = nb
            span = 2
            while span < GDN_INV_LEAF:
                pw = _Split(_dot3(pw, pw))
                yield
                tinv = tinv + _dot3(_Split(tinv, exact=tinv_exact), pw)
                tinv_exact = False
                span *= 2
            size = GDN_INV_LEAF
            while size < cl:
                yield
                ts = _Split(tinv)
                xm = _dot3(_Split(jnp.where(merge_masks[size], lmat, 0.0), exact=True), ts)
                yield
                tinv = tinv - _dot3(ts, _Split(xm))
                size *= 2
            yield
            ts = _Split(tinv)
            wu = (_dot(ts.hi, rhs) + _dot(ts.lo, rhs)).astype(BF16)
            yield
            aw = _dot(attn, wu)
            kw = _dot(kdt, wu)
            yield
            p_ref[hh, c] = (qd - aw[:, :hd]).astype(BF16)
            ou_ref[hh, c] = aw[:, hd:]
            m_ref[hh, c] = (-kw[:, :hd]).astype(BF16)
            r_ref[hh, c] = kw[:, hd:]
            egl_ref[hh, c] = jnp.broadcast_to(e_gl, (8, 128))

        prep_defs.append(prep_chunk)

    groups = n_chunks // GDN_GROUP
    slabs_per_group = GDN_GROUP * cl // CONV_SLAB

    def conv_tasks(hh, i):
        tasks = []
        for t in range(slabs_per_group):
            r0 = pl.multiple_of((i * slabs_per_group + t) * CONV_SLAB, CONV_SLAB)
            for which in range(3):
                tasks.append(conv_defs[hh](which, r0, len(tasks)))
        return tasks

    def conv_only(i, carry):
        _run_interleaved(conv_tasks(0, i))
        return carry

    lax.fori_loop(0, groups, conv_only, 0)

    for hh in range(hps):
        def stage_body(i, carry, hh=hh):
            tasks = [prep_defs[hh](i * GDN_GROUP + u) for u in range(GDN_GROUP)]
            if hh + 1 < hps:
                tasks += conv_tasks(hh + 1, i)
            _run_interleaved(tasks)
            return carry

        lax.fori_loop(0, groups, stage_body, 0)

    ng = ng_ref[...]

    def scan_body(c, states):
        r0 = pl.multiple_of(c * cl, cl)
        new_states = [None] * hps

        def head_step(hh):
            s = states[hh]
            p = p_ref[hh, c]
            m = m_ref[hh, c]
            ou = ou_ref[hh, c]
            r = r_ref[hh, c]
            eg = egl_ref[hh, c][0:1, :]
            gt = gate_ref[0, pl.ds(r0, cl), hh * hd:(hh + 1) * hd].astype(F32)
            yield
            sb = s.astype(BF16)
            o = _dot(p, sb) + ou
            new_states[hh] = eg * s + (_dot(m, sb) + r)
            rn = lax.rsqrt(jnp.mean(o * o, axis=-1, keepdims=True) + EPS)
            y = ((o * rn) * ng * _silu(gt)).astype(o_ref.dtype)
            yield
            o_ref[0, pl.ds(r0, cl), hh * hd:(hh + 1) * hd] = y

        _run_interleaved([head_step(hh) for hh in range(hps)])
        return tuple(new_states)

    init = tuple(jnp.zeros((hd, hd), F32) for _ in range(hps))
    lax.fori_loop(0, n_chunks, scan_body, init)


def _gdn_core(main, ab, cw, a_log, dt_bias, norm_g, bsz, seq):
    hd = GDN_HEAD_DIM
    nh = GDN_HEADS
    hps = GDN_HEADS_PER_STEP
    groups = nh // hps
    n_chunks = seq // GDN_CHUNK
    main3 = main.reshape(bsz, seq, main.shape[1])
    ab3 = ab.reshape(bsz, seq, 128)
    smem = pl.BlockSpec(memory_space=pltpu.SMEM)

    def col_spec(which):
        return pl.BlockSpec((1, seq, hps * hd), lambda b, h, which=which: (b, 0, which * groups + h))

    return pl.pallas_call(
        _gdn_core_kernel,
        grid=(bsz, groups),
        in_specs=[
            smem, smem,
            col_spec(0), col_spec(1), col_spec(2), col_spec(3),
            pl.BlockSpec((1, seq, 128), lambda b, h: (b, 0, 0)),
            pl.BlockSpec((hps, 3, GDN_CONV, hd), lambda b, h: (h, 0, 0, 0)),
            pl.BlockSpec((1, hd), lambda b, h: (0, 0)),
        ],
        out_specs=pl.BlockSpec((1, seq, hps * hd), lambda b, h: (b, 0, h)),
        out_shape=jax.ShapeDtypeStruct((bsz, seq, nh * hd), BF16),
        scratch_shapes=[
            pltpu.VMEM((2, 3, seq, hd), F32),
            pltpu.VMEM((hps, n_chunks, GDN_CHUNK, hd), BF16),
            pltpu.VMEM((hps, n_chunks, GDN_CHUNK, hd), F32),
            pltpu.VMEM((hps, n_chunks, hd, hd), BF16),
            pltpu.VMEM((hps, n_chunks, hd, hd), F32),
            pltpu.VMEM((hps, n_chunks, 8, 128), F32),
            pltpu.VMEM((GDN_CHUNK, GDN_CHUNK), BF16),
        ],
        compiler_params=_cparams(("parallel", "parallel")),
        name="gdn_core",
    )(a_log, dt_bias, main3, main3, main3, main3, ab3, cw, norm_g)


def _mla_proj_kernel(x_ref, g_ref, sc_ref, sh_ref, pos_ref, tab_ref, win_ref, qg_ref, kvg_ref,
                     wq_ref, wqs_ref, wkn_ref, wv_ref, q_ref, kn_ref, kr_ref, v_ref):
    h = _norm_mod(x_ref[...], g_ref[...], sc_ref[0], sh_ref[0]).astype(BF16)
    proj = _dot(h, win_ref[...])
    qr, kvr = MLA_Q_RANK, MLA_KV_RANK
    c_q = proj[:, :qr]
    c_kv = proj[:, qr:qr + kvr]
    kr_a = proj[:, qr + kvr:qr + kvr + 128]
    kr_b = proj[:, qr + kvr + 128:qr + kvr + 256]

    def rms(t, gain):
        return (t * lax.rsqrt(jnp.mean(t * t, axis=-1, keepdims=True) + EPS)) * gain

    cqn = rms(c_q, qg_ref[...])
    ckvn = rms(c_kv, kvg_ref[...])
    cqn_t = cqn.T.astype(BF16)
    ckvn_t = ckvn.T.astype(BF16)
    ckvn = ckvn.astype(BF16)

    ang = pos_ref[...].astype(F32) * tab_ref[0:1, :]
    ct = jnp.cos(ang) * tab_ref[1:2, :]
    st = jnp.sin(ang) * tab_ref[2:3, :]
    kr_ref[...] = (kr_a * ct + kr_b * st).astype(kr_ref.dtype)
    ct_t = ct.T[:MLA_ROPE, :]
    st_t = st.T[:MLA_ROPE, :]

    scale = (MLA_NOPE + MLA_ROPE) ** -0.5 * LOG2_E
    qf = _dot(wq_ref[...], cqn_t)
    qs = _dot(wqs_ref[...], cqn_t)
    for hh in range(MLA_HEADS):
        base = hh * MLA_QK_PAD
        rope0 = base + MLA_NOPE
        q_ref[0, base:rope0, :] = (qf[base:rope0, :] * scale).astype(q_ref.dtype)
        rot = qf[rope0:rope0 + MLA_ROPE, :] * ct_t + qs[hh * MLA_ROPE:(hh + 1) * MLA_ROPE, :] * st_t
        q_ref[0, rope0:rope0 + MLA_ROPE, :] = (rot * scale).astype(q_ref.dtype)
        q_ref[0, rope0 + MLA_ROPE:base + MLA_QK_PAD, :] = jnp.zeros(
            (MLA_QK_PAD - MLA_NOPE - MLA_ROPE, q_ref.shape[2]), q_ref.dtype)

    kn_ref[...] = _dot(ckvn, wkn_ref[...]).astype(kn_ref.dtype)
    v_ref[0] = _dot(wv_ref[...], ckvn_t).astype(v_ref.dtype)


def _mla_proj(x2, g, mod, pos2, tab, w_in, qg, kvg, wq, wqs, wkn, wv, seq):
    n, d = x2.shape
    tm = ROW_TILE
    per_b = seq // tm
    full = lambda a: pl.BlockSpec(a.shape, lambda i: (0,) * a.ndim)
    hq = MLA_HEADS * MLA_QK_PAD
    hv = MLA_HEADS * MLA_V
    return pl.pallas_call(
        _mla_proj_kernel,
        grid=(n // tm,),
        in_specs=[
            pl.BlockSpec((tm, d), lambda i: (i, 0)),
            full(g),
            pl.BlockSpec((1, 1, d), lambda i: ((i // per_b) * N_MOD + 1, 0, 0)),
            pl.BlockSpec((1, 1, d), lambda i: ((i // per_b) * N_MOD + 0, 0, 0)),
            pl.BlockSpec((tm, 1), lambda i: (i, 0)),
            full(tab), full(w_in), full(qg), full(kvg), full(wq), full(wqs), full(wkn), full(wv),
        ],
        out_specs=[
            pl.BlockSpec((1, hq, tm), lambda i: (i // per_b, 0, i % per_b)),
            pl.BlockSpec((tm, hv), lambda i: (i, 0)),
            pl.BlockSpec((tm, 128), lambda i: (i, 0)),
            pl.BlockSpec((1, hv, tm), lambda i: (i // per_b, 0, i % per_b)),
        ],
        out_shape=[
            jax.ShapeDtypeStruct((n // seq, hq, seq), BF16),
            jax.ShapeDtypeStruct((n, hv), BF16),
            jax.ShapeDtypeStruct((n, 128), BF16),
            jax.ShapeDtypeStruct((n // seq, hv, seq), BF16),
        ],
        compiler_params=_cparams(("parallel",)),
        name="mla_proj",
    )(x2, g, mod, mod, pos2, tab, w_in, qg, kvg, wq, wqs, wkn, wv)


def _flash_kernel(q_ref, kn_ref, kr_ref, v_ref, o_ref):
    qi = pl.program_id(2)
    blk = q_ref.shape[2]
    n_blocks = kn_ref.shape[1] // blk

    def attend(n_keys):
        q_t = q_ref[0]
        m = jnp.full((1, blk), -jnp.inf, F32)
        l = jnp.zeros((1, blk), F32)
        acc = jnp.zeros((MLA_V, blk), F32)
        for j in range(n_keys):
            ks = slice(j * blk, (j + 1) * blk)
            kcat = jnp.concatenate([kn_ref[0, ks, :], kr_ref[0, ks, :]], axis=1)
            s = _dot(kcat, q_t)
            if j == n_keys - 1:
                key = lax.broadcasted_iota(jnp.int32, s.shape, 0)
                qry = lax.broadcasted_iota(jnp.int32, s.shape, 1)
                s = jnp.where(key <= qry, s, -jnp.inf)
            m_new = jnp.maximum(m, jnp.max(s, axis=0, keepdims=True))
            alpha = jnp.exp2(m - m_new)
            p = jnp.exp2(s - m_new)
            l = alpha * l + jnp.sum(p, axis=0, keepdims=True)
            acc = alpha * acc + _dot(v_ref[0, :, ks], p.astype(BF16))
            m = m_new
        o_ref[0] = (acc / l).T.astype(o_ref.dtype)

    for i in range(n_blocks):
        pl.when(qi == i)(functools.partial(attend, i + 1))


def _flash_attention(q_t, kn, kr, v_t, bsz, seq):
    nh = MLA_HEADS
    blk = ATTN_BLOCK
    kn3 = kn.reshape(bsz, seq, nh * MLA_NOPE)
    kr3 = kr.reshape(bsz, seq, 128)
    return pl.pallas_call(
        _flash_kernel,
        grid=(bsz, nh, seq // blk),
        in_specs=[
            pl.BlockSpec((1, MLA_QK_PAD, blk), lambda b, h, i: (b, h, i)),
            pl.BlockSpec((1, seq, MLA_NOPE), lambda b, h, i: (b, 0, h)),
            pl.BlockSpec((1, seq, 128), lambda b, h, i: (b, 0, 0)),
            pl.BlockSpec((1, MLA_V, seq), lambda b, h, i: (b, h, 0)),
        ],
        out_specs=pl.BlockSpec((1, blk, MLA_V), lambda b, h, i: (b, i, h)),
        out_shape=jax.ShapeDtypeStruct((bsz, seq, nh * MLA_V), BF16),
        compiler_params=_cparams(("parallel", "parallel", "parallel")),
        name="mla_flash",
    )(q_t, kn3, kr3, v_t)


def _ffn_kernel(a_ref, wo_ref, x_ref, gm_ref, g_ref, sc_ref, sh_ref, gt_ref, wg_ref, wu_ref, wd_ref,
                fg_ref, o_ref, *, final):
    xm = x_ref[...] + gm_ref[0] * _dot(a_ref[...], wo_ref[...])
    h = _norm_mod(xm, g_ref[...], sc_ref[0], sh_ref[0]).astype(BF16)
    dff = wg_ref.shape[1]
    acc = jnp.zeros(xm.shape, F32)
    for f0 in range(0, dff, dff // FFN_SPLIT):
        cols = slice(f0, f0 + dff // FFN_SPLIT)
        act = (_silu(_dot(h, wg_ref[:, cols])) * _dot(h, wu_ref[:, cols])).astype(BF16)
        acc = acc + _dot(act, wd_ref[cols, :])
    xn = xm + gt_ref[0] * acc
    if final:
        xn = (xn * lax.rsqrt(jnp.mean(xn * xn, axis=-1, keepdims=True) + EPS)) * fg_ref[...]
    o_ref[...] = xn


def _mixer_out_ffn(a2, w_out, x2, g, mod, wg, wu, wd, fg, seq, final):
    n, d = x2.shape
    dff = wg.shape[1]
    tm = ROW_TILE
    per_b = seq // tm
    mod_spec = lambda k: pl.BlockSpec((1, 1, d), lambda i, k=k: ((i // per_b) * N_MOD + k, 0, 0))
    resident = lambda a: pl.BlockSpec(a.shape, lambda i: (0,) * a.ndim, pipeline_mode=pl.Buffered(1))
    return pl.pallas_call(
        functools.partial(_ffn_kernel, final=final),
        grid=(n // tm,),
        in_specs=[
            pl.BlockSpec((tm, a2.shape[1]), lambda i: (i, 0)),
            resident(w_out),
            pl.BlockSpec((tm, d), lambda i: (i, 0)),
            mod_spec(2),
            pl.BlockSpec((1, d), lambda i: (0, 0)),
            mod_spec(4), mod_spec(3), mod_spec(5),
            resident(wg), resident(wu), resident(wd),
            pl.BlockSpec((1, d), lambda i: (0, 0)),
        ],
        out_specs=pl.BlockSpec((tm, d), lambda i: (i, 0)),
        out_shape=jax.ShapeDtypeStruct((n, d), F32),
        compiler_params=_cparams(("parallel",)),
        name="ffn_final" if final else "ffn",
    )(a2, w_out, x2, mod, g, mod, mod, mod, wg, wu, wd, fg)


def _gdn_weights(w_in, conv_w):
    main_cols = 2 * GDN_HEADS * GDN_HEAD_DIM + 2 * GDN_HEADS * GDN_HEAD_DIM
    w_main = w_in[:, :main_cols].astype(BF16)
    w_ab = jnp.pad(w_in[:, main_cols:], ((0, 0), (0, 128 - 2 * GDN_HEADS))).astype(BF16)
    cw = conv_w.T.reshape(GDN_CONV, 3, GDN_HEADS, GDN_HEAD_DIM).transpose(2, 1, 0, 3)
    return w_main, w_ab, cw


def _mla_weights(w_in, w_uq, w_ukv):
    qr, kvr, half = MLA_Q_RANK, MLA_KV_RANK, MLA_ROPE // 2
    d = w_in.shape[0]
    rope = w_in[:, qr + kvr:]
    z = jnp.zeros((d, 128 - MLA_ROPE), w_in.dtype)
    w_in_ext = jnp.concatenate(
        [w_in[:, :qr + kvr], rope, z, rope[:, half:], rope[:, :half], z], axis=1).astype(BF16)
    uq = w_uq.reshape(qr, MLA_HEADS, MLA_NOPE + MLA_ROPE)
    nope, r = uq[..., :MLA_NOPE], uq[..., MLA_NOPE:]
    zq = jnp.zeros((qr, MLA_HEADS, 128 - MLA_ROPE), w_uq.dtype)
    wq = jnp.concatenate([nope, r, zq], axis=-1).reshape(qr, MLA_HEADS * MLA_QK_PAD).T.astype(BF16)
    wqs = jnp.concatenate([r[..., half:], r[..., :half]], axis=-1).reshape(
        qr, MLA_HEADS * MLA_ROPE).T.astype(BF16)
    ukv = w_ukv.reshape(kvr, MLA_HEADS, MLA_NOPE + MLA_V)
    wkn = ukv[..., :MLA_NOPE].reshape(kvr, MLA_HEADS * MLA_NOPE).astype(BF16)
    wv = ukv[..., MLA_NOPE:].reshape(kvr, MLA_HEADS * MLA_V).T.astype(BF16)
    return w_in_ext, wq, wqs, wkn, wv


def _rope_table():
    half = MLA_ROPE // 2
    inv_freq = ROPE_THETA ** (-jnp.arange(0, MLA_ROPE, 2, dtype=F32) / MLA_ROPE)
    z = jnp.zeros((128 - MLA_ROPE,), F32)
    ones = jnp.ones((half,), F32)
    freq = jnp.concatenate([inv_freq, inv_freq, z])
    cmask = jnp.concatenate([ones, ones, z])
    ssign = jnp.concatenate([-ones, ones, z])
    pad = jnp.zeros((5, 128), F32)
    return jnp.concatenate([jnp.stack([freq, cmask, ssign]), pad], axis=0)


def kernel(x, c, positions, ada_w, ada_b, norm_mix_g, norm_ffn_g, gdn_w_in, gdn_conv_w, gdn_a_log,
           gdn_dt_bias, gdn_norm_g, gdn_w_out, mla_w_in, mla_q_norm_g, mla_kv_norm_g, mla_w_uq,
           mla_w_ukv, mla_w_out, ffn_w_gate, ffn_w_up, ffn_w_down, final_norm_g):
    bsz, seq, d = x.shape
    depth = ada_w.shape[0]
    n_mixers = 2
    x2 = x.reshape(bsz * seq, d)
    pos2 = positions.reshape(bsz * seq, 1)
    tab = _rope_table()

    mod_all = _adaln(c, ada_w, ada_b).reshape(depth, bsz * N_MOD, 1, d)

    for layer in range(depth):
        mod = mod_all[layer]
        j = layer // n_mixers
        g_mix = norm_mix_g[layer].reshape(1, d)
        if layer % n_mixers == 0:
            w_main, w_ab, cw = _gdn_weights(gdn_w_in[j], gdn_conv_w[j])
            main, ab = _gdn_inproj(x2, g_mix, mod, w_main, w_ab, seq)
            o = _gdn_core(main, ab, cw, gdn_a_log[j], gdn_dt_bias[j],
                          gdn_norm_g[j].reshape(1, GDN_HEAD_DIM), bsz, seq)
            w_out = gdn_w_out[j].astype(BF16)
        else:
            w_in_ext, wq, wqs, wkn, wv = _mla_weights(mla_w_in[j], mla_w_uq[j], mla_w_ukv[j])
            qcat, kn, kr, v = _mla_proj(
                x2, g_mix, mod, pos2, tab, w_in_ext,
                mla_q_norm_g[j].reshape(1, -1), mla_kv_norm_g[j].reshape(1, -1),
                wq, wqs, wkn, wv, seq)
            o = _flash_attention(qcat, kn, kr, v, bsz, seq)
            w_out = mla_w_out[j].astype(BF16)

        x2 = _mixer_out_ffn(o.reshape(bsz * seq, -1), w_out, x2, norm_ffn_g[layer].reshape(1, d), mod,
                            ffn_w_gate[layer].astype(BF16), ffn_w_up[layer].astype(BF16),
                            ffn_w_down[layer].astype(BF16), final_norm_g.reshape(1, d), seq,
                            final=(layer == depth - 1))

    return x2.reshape(bsz, seq, d)
```

```python
import functools

import jax
import jax.numpy as jnp
from jax import lax
from jax.experimental import pallas as pl
from jax.experimental.pallas import tpu as pltpu

F32 = jnp.float32
BF16 = jnp.bfloat16

D_MODEL = 1024
N_MOD = 6
EPS = 1e-6

GDN_HEADS = 8
GDN_HEAD_DIM = 128
GDN_CONV = 4
GDN_CHUNK = 64
GDN_INPROJ_COLS = 1024
GDN_INV_LEAF = 8
GDN_HEADS_PER_STEP = 4
GDN_GROUP = 16
CONV_SLAB = 256
CONV_HALO = 16

MLA_HEADS = 8
MLA_NOPE = 128
MLA_ROPE = 64
MLA_V = 128
MLA_Q_RANK = 384
MLA_KV_RANK = 256
MLA_QK_PAD = 256
ROPE_THETA = 10000.0

ROW_TILE = 512
FFN_SPLIT = 2
ATTN_BLOCK = 512
ATTN_KEY_BLOCK = 256
LOG2_E = 1.4426950408889634
VMEM_LIMIT = 56 * 1024 * 1024


def _cparams(sem, vmem=VMEM_LIMIT):
    return pltpu.CompilerParams(dimension_semantics=sem, vmem_limit_bytes=vmem)


def _sigmoid(x):
    return 1.0 / (1.0 + jnp.exp(-x))


def _silu(x):
    return x * _sigmoid(x)


def _softplus(x):
    return jnp.maximum(x, 0.0) + jnp.log(1.0 + jnp.exp(-jnp.abs(x)))


def _dot(a, b):
    return jnp.dot(a, b, preferred_element_type=F32)


def _dot_nt(a, b):
    return lax.dot_general(a, b, (((1,), (1,)), ((), ())), preferred_element_type=F32)


def _split(a):
    hi = a.astype(BF16)
    lo = (a - hi.astype(F32)).astype(BF16)
    return hi, lo


class _Split:
    def __init__(self, a, exact=False):
        self.f32 = a
        if exact:
            self.hi, self.lo = a.astype(BF16), None
        else:
            self.hi, self.lo = _split(a)


def _dot3(a, b):
    out = _dot(a.hi, b.hi)
    if b.lo is not None:
        out = out + _dot(a.hi, b.lo)
    if a.lo is not None:
        out = out + _dot(a.lo, b.hi)
    return out


def _norm_mod(x, g, scale, shift):
    r = lax.rsqrt(jnp.mean(x * x, axis=-1, keepdims=True) + EPS)
    return (x * r) * g * (1.0 + scale) + shift


def _adaln_kernel(c_ref, w_ref, b_ref, o_ref):
    c_act = _silu(c_ref[...]).astype(BF16)
    o_ref[0] = _dot(c_act, w_ref[0].astype(BF16)) + b_ref[0]


def _adaln(c, ada_w, ada_b):
    depth, d, n = ada_w.shape
    bsz = c.shape[0]
    tn = 1536
    return pl.pallas_call(
        _adaln_kernel,
        grid=(depth, n // tn),
        in_specs=[
            pl.BlockSpec((bsz, d), lambda l, j: (0, 0)),
            pl.BlockSpec((1, d, tn), lambda l, j: (l, 0, j)),
            pl.BlockSpec((1, 1, tn), lambda l, j: (l, 0, j)),
        ],
        out_specs=pl.BlockSpec((1, bsz, tn), lambda l, j: (l, 0, j)),
        out_shape=jax.ShapeDtypeStruct((depth, bsz, n), F32),
        compiler_params=_cparams(("parallel", "parallel")),
        name="adaln_mod",
    )(c, ada_w, ada_b.reshape(depth, 1, n))


def _gdn_inproj_kernel(x_ref, g_ref, sc_ref, sh_ref, w_ref, wab_ref, o_ref, ab_ref):
    h = _norm_mod(x_ref[...], g_ref[...], sc_ref[0], sh_ref[0]).astype(BF16)
    ab_ref[...] = _dot(h, wab_ref[...])
    nout = o_ref.shape[1]
    for c0 in range(0, nout, GDN_INPROJ_COLS):
        cols = slice(c0, c0 + GDN_INPROJ_COLS)
        o_ref[:, cols] = _dot(h, w_ref[:, cols]).astype(o_ref.dtype)


def _gdn_inproj(x2, g, mod, w_main, w_ab, seq):
    n, d = x2.shape
    nout = w_main.shape[1]
    tm = ROW_TILE
    per_b = seq // tm
    return pl.pallas_call(
        _gdn_inproj_kernel,
        grid=(n // tm,),
        in_specs=[
            pl.BlockSpec((tm, d), lambda i: (i, 0)),
            pl.BlockSpec((1, d), lambda i: (0, 0)),
            pl.BlockSpec((1, 1, d), lambda i: ((i // per_b) * N_MOD + 1, 0, 0)),
            pl.BlockSpec((1, 1, d), lambda i: ((i // per_b) * N_MOD + 0, 0, 0)),
            pl.BlockSpec((d, nout), lambda i: (0, 0)),
            pl.BlockSpec((d, 128), lambda i: (0, 0)),
        ],
        out_specs=[
            pl.BlockSpec((tm, nout), lambda i: (i, 0)),
            pl.BlockSpec((tm, 128), lambda i: (i, 0)),
        ],
        out_shape=[
            jax.ShapeDtypeStruct((n, nout), BF16),
            jax.ShapeDtypeStruct((n, 128), F32),
        ],
        compiler_params=_cparams(("parallel",)),
        name="gdn_inproj",
    )(x2, g, mod, mod, w_main, w_ab)


def _run_interleaved(gens):
    while gens:
        alive = []
        for gen in gens:
            try:
                next(gen)
                alive.append(gen)
            except StopIteration:
                pass
        gens = alive


def _gdn_core_kernel(alog_ref, dtb_ref, q_ref, k_ref, v_ref, gate_ref, ab_ref, cw_ref, ng_ref,
                     o_ref, xc_ref, p_ref, ou_ref, m_ref, r_ref, egl_ref, tril_ref):
    hps = GDN_HEADS_PER_STEP
    hd = GDN_HEAD_DIM
    head0 = pl.program_id(1) * hps
    seq = q_ref.shape[1]
    cl = GDN_CHUNK
    n_chunks = seq // cl

    row = lax.broadcasted_iota(jnp.int32, (cl, cl), 0)
    col = lax.broadcasted_iota(jnp.int32, (cl, cl), 1)
    tril = row >= col
    strict = row > col
    tril_ref[...] = tril.astype(F32).astype(BF16)
    eye = (row == col).astype(F32)
    lane = lax.broadcasted_iota(jnp.int32, (cl, 128), 1)
    def same_block(size):
        shift = size.bit_length() - 1
        return jnp.right_shift(row, shift) == jnp.right_shift(col, shift)

    leaf_mask = same_block(GDN_INV_LEAF)
    merge_masks = {}
    size = GDN_INV_LEAF
    while size < cl:
        merge_masks[size] = same_block(2 * size) & jnp.logical_not(same_block(size))
        size *= 2

    conv_defs, prep_defs = [], []
    for hh in range(hps):
        head = head0 + hh
        lanes = slice(hh * hd, (hh + 1) * hd)

        def conv_slab(which, r0, delay, hh=hh, lanes=lanes):
            src_ref = (q_ref, k_ref, v_ref)[which]
            body = src_ref[0, pl.ds(r0, CONV_SLAB), lanes].astype(F32)
            h0 = pl.multiple_of(jnp.maximum(r0 - CONV_HALO, 0), CONV_HALO)
            halo = src_ref[0, pl.ds(h0, CONV_HALO), lanes].astype(F32)
            halo = jnp.where(r0 > 0, halo, 0.0)
            ext = jnp.concatenate([halo, body], axis=0)
            w = cw_ref[hh, which]
            for _ in range(1 + delay):
                yield
            acc = ext[CONV_HALO:] * w[GDN_CONV - 1:GDN_CONV]
            for sh in range(1, GDN_CONV):
                tap = GDN_CONV - 1 - sh
                acc = acc + pltpu.roll(ext, sh, 0)[CONV_HALO:] * w[tap:tap + 1]
            y = _silu(acc)
            if which < 2:
                y = y * lax.rsqrt(jnp.sum(y * y, axis=-1, keepdims=True) + EPS)
            if which == 0:
                y = y * (hd ** -0.5)
            yield
            xc_ref[hh % 2, which, pl.ds(r0, CONV_SLAB), :] = y

        conv_defs.append(conv_slab)

        a_head = jnp.exp(jnp.full((1, 1), alog_ref[head], F32))
        dt_head = dtb_ref[head]

        def prep_chunk(c, hh=hh, head=head, a_head=a_head, dt_head=dt_head):
            r0 = pl.multiple_of(c * cl, cl)
            q = xc_ref[hh % 2, 0, pl.ds(r0, cl), :]
            k = xc_ref[hh % 2, 1, pl.ds(r0, cl), :]
            v = xc_ref[hh % 2, 2, pl.ds(r0, cl), :]
            ab = ab_ref[0, pl.ds(r0, cl), :]
            yield
            a_col = jnp.sum(jnp.where(lane == head, ab, 0.0), axis=1, keepdims=True)
            b_col = jnp.sum(jnp.where(lane == head + GDN_HEADS, ab, 0.0), axis=1, keepdims=True)
            g = -a_head * _softplus(a_col + dt_head)
            beta = _sigmoid(b_col)
            g_hi, g_lo = _split(jnp.broadcast_to(g, (cl, 128)))
            tril_b = tril_ref[...]
            gc = _dot(tril_b, g_hi) + _dot(tril_b, g_lo)
            kb = k * beta
            k_b16 = k.astype(BF16)
            kk = _dot_nt(kb.astype(BF16), k_b16)
            qk = _dot_nt(q.astype(BF16), k_b16)
            yield
            gl = gc[cl - 1:cl, :]
            e_gc = jnp.exp(gc)
            e_rev = jnp.exp(gl - gc)
            e_gl = jnp.exp(gl)
            diff = gc[:, :cl] - gc[:, :cl].T
            dec = jnp.exp(jnp.where(tril, diff, -jnp.inf))
            lmat = jnp.where(strict, kk * dec, 0.0).astype(BF16).astype(F32)
            attn = (qk * dec).astype(BF16)
            rhs = jnp.concatenate([kb * e_gc, v * beta], axis=1).astype(BF16)
            kdt = (k * e_rev).T.astype(BF16)
            qd = q * e_gc
            nb = _Split(-jnp.where(leaf_mask, lmat, 0.0), exact=True)
            tinv = eye + nb.f32
            tinv_exact = True
            pw = nb
            span = 2
            while span < GDN_INV_LEAF:
                pw = _Split(_dot3(pw, pw))
                yield
                tinv = tinv + _dot3(_Split(tinv, exact=tinv_exact), pw)
                tinv_exact = False
                span *= 2
            size = GDN_INV_LEAF
            while size < cl:
                yield
                ts = _Split(tinv)
                xm = _dot3(_Split(jnp.where(merge_masks[size], lmat, 0.0), exact=True), ts)
                yield
                tinv = tinv - _dot3(ts, _Split(xm))
                size *= 2
            yield
            ts = _Split(tinv)
            wu = (_dot(ts.hi, rhs) + _dot(ts.lo, rhs)).astype(BF16)
            yield
            aw = _dot(attn, wu)
            kw = _dot(kdt, wu)
            yield
            p_ref[hh, c] = (qd - aw[:, :hd]).astype(BF16)
            ou_ref[hh, c] = aw[:, hd:]
            m_ref[hh, c] = (-kw[:, :hd]).astype(BF16)
            r_ref[hh, c] = kw[:, hd:]
            egl_ref[hh, c] = jnp.broadcast_to(e_gl, (8, 128))

        prep_defs.append(prep_chunk)

    groups = n_chunks // GDN_GROUP
    slabs_per_group = GDN_GROUP * cl // CONV_SLAB

    def conv_tasks(hh, i):
        tasks = []
        for t in range(slabs_per_group):
            r0 = pl.multiple_of((i * slabs_per_group + t) * CONV_SLAB, CONV_SLAB)
            for which in range(3):
                tasks.append(conv_defs[hh](which, r0, len(tasks)))
        return tasks

    def conv_only(i, carry):
        _run_interleaved(conv_tasks(0, i))
        return carry

    lax.fori_loop(0, groups, conv_only, 0)

    for hh in range(hps):
        def stage_body(i, carry, hh=hh):
            tasks = [prep_defs[hh](i * GDN_GROUP + u) for u in range(GDN_GROUP)]
            if hh + 1 < hps:
                tasks += conv_tasks(hh + 1, i)
            _run_interleaved(tasks)
            return carry

        lax.fori_loop(0, groups, stage_body, 0)

    ng = ng_ref[...]

    def scan_body(c, states):
        r0 = pl.multiple_of(c * cl, cl)
        new_states = [None] * hps

        def head_step(hh):
            s = states[hh]
            p = p_ref[hh, c]
            m = m_ref[hh, c]
            ou = ou_ref[hh, c]
            r = r_ref[hh, c]
            eg = egl_ref[hh, c][0:1, :]
            gt = gate_ref[0, pl.ds(r0, cl), hh * hd:(hh + 1) * hd].astype(F32)
            yield
            sb = s.astype(BF16)
            o = _dot(p, sb) + ou
            new_states[hh] = eg * s + (_dot(m, sb) + r)
            rn = lax.rsqrt(jnp.mean(o * o, axis=-1, keepdims=True) + EPS)
            y = ((o * rn) * ng * _silu(gt)).astype(o_ref.dtype)
            yield
            o_ref[0, pl.ds(r0, cl), hh * hd:(hh + 1) * hd] = y

        _run_interleaved([head_step(hh) for hh in range(hps)])
        return tuple(new_states)

    init = tuple(jnp.zeros((hd, hd), F32) for _ in range(hps))
    lax.fori_loop(0, n_chunks, scan_body, init)


def _gdn_core(main, ab, cw, a_log, dt_bias, norm_g, bsz, seq):
    hd = GDN_HEAD_DIM
    nh = GDN_HEADS
    hps = GDN_HEADS_PER_STEP
    groups = nh // hps
    n_chunks = seq // GDN_CHUNK
    main3 = main.reshape(bsz, seq, main.shape[1])
    ab3 = ab.reshape(bsz, seq, 128)
    smem = pl.BlockSpec(memory_space=pltpu.SMEM)

    def col_spec(which):
        return pl.BlockSpec((1, seq, hps * hd), lambda b, h, which=which: (b, 0, which * groups + h))

    return pl.pallas_call(
        _gdn_core_kernel,
        grid=(bsz, groups),
        in_specs=[
            smem, smem,
            col_spec(0), col_spec(1), col_spec(2), col_spec(3),
            pl.BlockSpec((1, seq, 128), lambda b, h: (b, 0, 0)),
            pl.BlockSpec((hps, 3, GDN_CONV, hd), lambda b, h: (h, 0, 0, 0)),
            pl.BlockSpec((1, hd), lambda b, h: (0, 0)),
        ],
        out_specs=pl.BlockSpec((1, seq, hps * hd), lambda b, h: (b, 0, h)),
        out_shape=jax.ShapeDtypeStruct((bsz, seq, nh * hd), BF16),
        scratch_shapes=[
            pltpu.VMEM((2, 3, seq, hd), F32),
            pltpu.VMEM((hps, n_chunks, GDN_CHUNK, hd), BF16),
            pltpu.VMEM((hps, n_chunks, GDN_CHUNK, hd), F32),
            pltpu.VMEM((hps, n_chunks, hd, hd), BF16),
            pltpu.VMEM((hps, n_chunks, hd, hd), F32),
            pltpu.VMEM((hps, n_chunks, 8, 128), F32),
            pltpu.VMEM((GDN_CHUNK, GDN_CHUNK), BF16),
        ],
        compiler_params=_cparams(("parallel", "parallel")),
        name="gdn_core",
    )(a_log, dt_bias, main3, main3, main3, main3, ab3, cw, norm_g)


def _mla_proj_kernel(x_ref, g_ref, sc_ref, sh_ref, pos_ref, tab_ref, win_ref, qg_ref, kvg_ref,
                     wq_ref, wqs_ref, wkn_ref, wv_ref, q_ref, kn_ref, kr_ref, v_ref):
    h = _norm_mod(x_ref[...], g_ref[...], sc_ref[0], sh_ref[0]).astype(BF16)
    proj = _dot(h, win_ref[...])
    qr, kvr = MLA_Q_RANK, MLA_KV_RANK
    c_q = proj[:, :qr]
    c_kv = proj[:, qr:qr + kvr]
    kr_a = proj[:, qr + kvr:qr + kvr + 128]
    kr_b = proj[:, qr + kvr + 128:qr + kvr + 256]

    def rms(t, gain):
        return (t * lax.rsqrt(jnp.mean(t * t, axis=-1, keepdims=True) + EPS)) * gain

    cqn = rms(c_q, qg_ref[...])
    ckvn = rms(c_kv, kvg_ref[...])
    cqn_t = cqn.T.astype(BF16)
    ckvn_t = ckvn.T.astype(BF16)
    ckvn = ckvn.astype(BF16)

    ang = pos_ref[...].astype(F32) * tab_ref[0:1, :]
    ct = jnp.cos(ang) * tab_ref[1:2, :]
    st = jnp.sin(ang) * tab_ref[2:3, :]
    kr_ref[...] = (kr_a * ct + kr_b * st).astype(kr_ref.dtype)
    ct_t = ct.T[:MLA_ROPE, :]
    st_t = st.T[:MLA_ROPE, :]

    scale = (MLA_NOPE + MLA_ROPE) ** -0.5 * LOG2_E
    qf = _dot(wq_ref[...], cqn_t)
    qs = _dot(wqs_ref[...], cqn_t)
    for hh in range(MLA_HEADS):
        base = hh * MLA_QK_PAD
        rope0 = base + MLA_NOPE
        q_ref[0, base:rope0, :] = (qf[base:rope0, :] * scale).astype(q_ref.dtype)
        rot = qf[rope0:rope0 + MLA_ROPE, :] * ct_t + qs[hh * MLA_ROPE:(hh + 1) * MLA_ROPE, :] * st_t
        q_ref[0, rope0:rope0 + MLA_ROPE, :] = (rot * scale).astype(q_ref.dtype)
        q_ref[0, rope0 + MLA_ROPE:base + MLA_QK_PAD, :] = jnp.zeros(
            (MLA_QK_PAD - MLA_NOPE - MLA_ROPE, q_ref.shape[2]), q_ref.dtype)

    kn_ref[...] = _dot(ckvn, wkn_ref[...]).astype(kn_ref.dtype)
    v_ref[0] = _dot(wv_ref[...], ckvn_t).astype(v_ref.dtype)


def _mla_proj(x2, g, mod, pos2, tab, w_in, qg, kvg, wq, wqs, wkn, wv, seq):
    n, d = x2.shape
    tm = ROW_TILE
    per_b = seq // tm
    full = lambda a: pl.BlockSpec(a.shape, lambda i: (0,) * a.ndim)
    hq = MLA_HEADS * MLA_QK_PAD
    hv = MLA_HEADS * MLA_V
    return pl.pallas_call(
        _mla_proj_kernel,
        grid=(n // tm,),
        in_specs=[
            pl.BlockSpec((tm, d), lambda i: (i, 0)),
            full(g),
            pl.BlockSpec((1, 1, d), lambda i: ((i // per_b) * N_MOD + 1, 0, 0)),
            pl.BlockSpec((1, 1, d), lambda i: ((i // per_b) * N_MOD + 0, 0, 0)),
            pl.BlockSpec((tm, 1), lambda i: (i, 0)),
            full(tab), full(w_in), full(qg), full(kvg), full(wq), full(wqs), full(wkn), full(wv),
        ],
        out_specs=[
            pl.BlockSpec((1, hq, tm), lambda i: (i // per_b, 0, i % per_b)),
            pl.BlockSpec((tm, hv), lambda i: (i, 0)),
            pl.BlockSpec((tm, 128), lambda i: (i, 0)),
            pl.BlockSpec((1, hv, tm), lambda i: (i // per_b, 0, i % per_b)),
        ],
        out_shape=[
            jax.ShapeDtypeStruct((n // seq, hq, seq), BF16),
            jax.ShapeDtypeStruct((n, hv), BF16),
            jax.ShapeDtypeStruct((n, 128), BF16),
            jax.ShapeDtypeStruct((n // seq, hv, seq), BF16),
        ],
        compiler_params=_cparams(("parallel",)),
        name="mla_proj",
    )(x2, g, mod, mod, pos2, tab, w_in, qg, kvg, wq, wqs, wkn, wv)


def _flash_kernel(q_ref, kn_ref, kr_ref, v_ref, o_ref):
    blk = ATTN_BLOCK
    n_blocks = kn_ref.shape[1] // blk

    def attend(qi):
        n_keys = qi + 1
        q_t = q_ref[0, :, qi * blk:(qi + 1) * blk]
        m = jnp.full((1, blk), -jnp.inf, F32)
        l = jnp.zeros((1, blk), F32)
        acc = jnp.zeros((MLA_V, blk), F32)
        kb = ATTN_KEY_BLOCK
        n_sub = n_keys * (blk // kb)
        diag0 = (n_keys - 1) * blk

        def logits(j):
            ks = slice(j * kb, (j + 1) * kb)
            kcat = jnp.concatenate([kn_ref[0, ks, :], kr_ref[0, ks, :]], axis=1)
            return _dot(kcat, q_t)

        s_next = logits(0)
        for j in range(n_sub):
            s = s_next
            if j + 1 < n_sub:
                s_next = logits(j + 1)
            if (j + 1) * kb > diag0:
                key = lax.broadcasted_iota(jnp.int32, s.shape, 0) + (j * kb - diag0)
                qry = lax.broadcasted_iota(jnp.int32, s.shape, 1)
                s = jnp.where(key <= qry, s, -jnp.inf)
            m_new = jnp.maximum(m, jnp.max(s, axis=0, keepdims=True))
            alpha = jnp.exp2(m - m_new)
            p = jnp.exp2(s - m_new)
            l = alpha * l + jnp.sum(p, axis=0, keepdims=True)
            acc = alpha * acc + _dot(v_ref[0, :, j * kb:(j + 1) * kb], p.astype(BF16))
            m = m_new
        o_ref[0, qi * blk:(qi + 1) * blk, :] = (acc / l).T.astype(o_ref.dtype)

    for i in range(n_blocks):
        attend(i)


def _flash_attention(q_t, kn, kr, v_t, bsz, seq):
    nh = MLA_HEADS
    blk = ATTN_BLOCK
    kn3 = kn.reshape(bsz, seq, nh * MLA_NOPE)
    kr3 = kr.reshape(bsz, seq, 128)
    return pl.pallas_call(
        _flash_kernel,
        grid=(bsz, nh),
        in_specs=[
            pl.BlockSpec((1, MLA_QK_PAD, seq), lambda b, h: (b, h, 0)),
            pl.BlockSpec((1, seq, MLA_NOPE), lambda b, h: (b, 0, h)),
            pl.BlockSpec((1, seq, 128), lambda b, h: (b, 0, 0)),
            pl.BlockSpec((1, MLA_V, seq), lambda b, h: (b, h, 0)),
        ],
        out_specs=pl.BlockSpec((1, seq, MLA_V), lambda b, h: (b, 0, h)),
        out_shape=jax.ShapeDtypeStruct((bsz, seq, nh * MLA_V), BF16),
        compiler_params=_cparams(("parallel", "parallel")),
        name="mla_flash",
    )(q_t, kn3, kr3, v_t)


def _ffn_kernel(a_ref, wo_ref, x_ref, gm_ref, g_ref, sc_ref, sh_ref, gt_ref, wg_ref, wu_ref, wd_ref,
                fg_ref, o_ref, *, final):
    xm = x_ref[...] + gm_ref[0] * _dot(a_ref[...], wo_ref[...])
    h = _norm_mod(xm, g_ref[...], sc_ref[0], sh_ref[0]).astype(BF16)
    dff = wg_ref.shape[1]
    acc = jnp.zeros(xm.shape, F32)
    for f0 in range(0, dff, dff // FFN_SPLIT):
        cols = slice(f0, f0 + dff // FFN_SPLIT)
        act = (_silu(_dot(h, wg_ref[:, cols])) * _dot(h, wu_ref[:, cols])).astype(BF16)
        acc = acc + _dot(act, wd_ref[cols, :])
    xn = xm + gt_ref[0] * acc
    if final:
        xn = (xn * lax.rsqrt(jnp.mean(xn * xn, axis=-1, keepdims=True) + EPS)) * fg_ref[...]
    o_ref[...] = xn


def _mixer_out_ffn(a2, w_out, x2, g, mod, wg, wu, wd, layer, fg, seq, final):
    n, d = x2.shape
    tm = ROW_TILE
    per_b = seq // tm
    mod_spec = lambda k: pl.BlockSpec((1, 1, d), lambda i, k=k: ((i // per_b) * N_MOD + k, 0, 0))
    resident = lambda a: pl.BlockSpec(a.shape, lambda i: (0,) * a.ndim, pipeline_mode=pl.Buffered(1))
    layer_resident = lambda a: pl.BlockSpec((None,) + a.shape[1:], lambda i: (layer, 0, 0),
                                            pipeline_mode=pl.Buffered(1))
    return pl.pallas_call(
        functools.partial(_ffn_kernel, final=final),
        grid=(n // tm,),
        in_specs=[
            pl.BlockSpec((tm, a2.shape[1]), lambda i: (i, 0)),
            resident(w_out),
            pl.BlockSpec((tm, d), lambda i: (i, 0)),
            mod_spec(2),
            pl.BlockSpec((1, d), lambda i: (0, 0)),
            mod_spec(4), mod_spec(3), mod_spec(5),
            layer_resident(wg), layer_resident(wu), layer_resident(wd),
            pl.BlockSpec((1, d), lambda i: (0, 0)),
        ],
        out_specs=pl.BlockSpec((tm, d), lambda i: (i, 0)),
        out_shape=jax.ShapeDtypeStruct((n, d), F32),
        compiler_params=_cparams(("parallel",)),
        name="ffn_final" if final else "ffn",
    )(a2, w_out, x2, mod, g, mod, mod, mod, wg, wu, wd, fg)


def _gdn_weights(w_in, conv_w):
    main_cols = 2 * GDN_HEADS * GDN_HEAD_DIM + 2 * GDN_HEADS * GDN_HEAD_DIM
    w_main = w_in[:, :main_cols].astype(BF16)
    w_ab = jnp.pad(w_in[:, main_cols:], ((0, 0), (0, 128 - 2 * GDN_HEADS))).astype(BF16)
    cw = conv_w.T.reshape(GDN_CONV, 3, GDN_HEADS, GDN_HEAD_DIM).transpose(2, 1, 0, 3)
    return w_main, w_ab, cw


def _mla_weights(w_in, w_uq, w_ukv):
    qr, kvr, half = MLA_Q_RANK, MLA_KV_RANK, MLA_ROPE // 2
    d = w_in.shape[0]
    rope = w_in[:, qr + kvr:]
    z = jnp.zeros((d, 128 - MLA_ROPE), w_in.dtype)
    w_in_ext = jnp.concatenate(
        [w_in[:, :qr + kvr], rope, z, rope[:, half:], rope[:, :half], z], axis=1).astype(BF16)
    uq = w_uq.reshape(qr, MLA_HEADS, MLA_NOPE + MLA_ROPE)
    nope, r = uq[..., :MLA_NOPE], uq[..., MLA_NOPE:]
    zq = jnp.zeros((qr, MLA_HEADS, 128 - MLA_ROPE), w_uq.dtype)
    wq = jnp.concatenate([nope, r, zq], axis=-1).reshape(qr, MLA_HEADS * MLA_QK_PAD).T.astype(BF16)
    wqs = jnp.concatenate([r[..., half:], r[..., :half]], axis=-1).reshape(
        qr, MLA_HEADS * MLA_ROPE).T.astype(BF16)
    ukv = w_ukv.reshape(kvr, MLA_HEADS, MLA_NOPE + MLA_V)
    wkn = ukv[..., :MLA_NOPE].reshape(kvr, MLA_HEADS * MLA_NOPE).astype(BF16)
    wv = ukv[..., MLA_NOPE:].reshape(kvr, MLA_HEADS * MLA_V).T.astype(BF16)
    return w_in_ext, wq, wqs, wkn, wv


def _rope_table():
    half = MLA_ROPE // 2
    inv_freq = ROPE_THETA ** (-jnp.arange(0, MLA_ROPE, 2, dtype=F32) / MLA_ROPE)
    z = jnp.zeros((128 - MLA_ROPE,), F32)
    ones = jnp.ones((half,), F32)
    freq = jnp.concatenate([inv_freq, inv_freq, z])
    cmask = jnp.concatenate([ones, ones, z])
    ssign = jnp.concatenate([-ones, ones, z])
    pad = jnp.zeros((5, 128), F32)
    return jnp.concatenate([jnp.stack([freq, cmask, ssign]), pad], axis=0)


def kernel(x, c, positions, ada_w, ada_b, norm_mix_g, norm_ffn_g, gdn_w_in, gdn_conv_w, gdn_a_log,
           gdn_dt_bias, gdn_norm_g, gdn_w_out, mla_w_in, mla_q_norm_g, mla_kv_norm_g, mla_w_uq,
           mla_w_ukv, mla_w_out, ffn_w_gate, ffn_w_up, ffn_w_down, final_norm_g):
    bsz, seq, d = x.shape
    depth = ada_w.shape[0]
    n_mixers = 2
    x2 = x.reshape(bsz * seq, d)
    pos2 = positions.reshape(bsz * seq, 1)
    tab = _rope_table()

    mod_all = _adaln(c, ada_w, ada_b).reshape(depth, bsz * N_MOD, 1, d)
    ffn_wg, ffn_wu, ffn_wd = (w.astype(BF16) for w in (ffn_w_gate, ffn_w_up, ffn_w_down))

    for layer in range(depth):
        mod = mod_all[layer]
        j = layer // n_mixers
        g_mix = norm_mix_g[layer].reshape(1, d)
        if layer % n_mixers == 0:
            w_main, w_ab, cw = _gdn_weights(gdn_w_in[j], gdn_conv_w[j])
            main, ab = _gdn_inproj(x2, g_mix, mod, w_main, w_ab, seq)
            o = _gdn_core(main, ab, cw, gdn_a_log[j], gdn_dt_bias[j],
                          gdn_norm_g[j].reshape(1, GDN_HEAD_DIM), bsz, seq)
            w_out = gdn_w_out[j].astype(BF16)
        else:
            w_in_ext, wq, wqs, wkn, wv = _mla_weights(mla_w_in[j], mla_w_uq[j], mla_w_ukv[j])
            qcat, kn, kr, v = _mla_proj(
                x2, g_mix, mod, pos2, tab, w_in_ext,
                mla_q_norm_g[j].reshape(1, -1), mla_kv_norm_g[j].reshape(1, -1),
                wq, wqs, wkn, wv, seq)
            o = _flash_attention(qcat, kn, kr, v, bsz, seq)
            w_out = mla_w_out[j].astype(BF16)

        x2 = _mixer_out_ffn(o.reshape(bsz * seq, -1), w_out, x2, norm_ffn_g[layer].reshape(1, d), mod,
                            ffn_wg, ffn_wu, ffn_wd, layer, final_norm_g.reshape(1, d), seq,
                            final=(layer == depth - 1))

    return x2.reshape(bsz, seq, d)
```

```python
import functools

import jax
import jax.numpy as jnp
from jax import lax
from jax.experimental import pallas as pl
from jax.experimental.pallas import tpu as pltpu

F32 = jnp.float32
BF16 = jnp.bfloat16

D_MODEL = 1024
N_MOD = 6
EPS = 1e-6

GDN_HEADS = 8
GDN_HEAD_DIM = 128
GDN_CONV = 4
GDN_CHUNK = 64
GDN_INPROJ_COLS = 1024
GDN_INV_LEAF = 8
GDN_HEADS_PER_STEP = 4
GDN_GROUP = 16
CONV_SLAB = 256
CONV_HALO = 16

MLA_HEADS = 8
MLA_NOPE = 128
MLA_ROPE = 64
MLA_V = 128
MLA_Q_RANK = 384
MLA_KV_RANK = 256
MLA_QK_PAD = 256
ROPE_THETA = 10000.0

ROW_TILE = 512
FFN_SPLIT = 1
ATTN_BLOCK = 512
ATTN_KEY_BLOCK = 256
LOG2_E = 1.4426950408889634
VMEM_LIMIT = 56 * 1024 * 1024


def _cparams(sem, vmem=VMEM_LIMIT):
    return pltpu.CompilerParams(dimension_semantics=sem, vmem_limit_bytes=vmem)


def _sigmoid(x):
    return 1.0 / (1.0 + jnp.exp(-x))


def _silu(x):
    return x * _sigmoid(x)


def _softplus(x):
    return jnp.maximum(x, 0.0) + jnp.log(1.0 + jnp.exp(-jnp.abs(x)))


def _dot(a, b):
    return jnp.dot(a, b, preferred_element_type=F32)


def _dot_nt(a, b):
    return lax.dot_general(a, b, (((1,), (1,)), ((), ())), preferred_element_type=F32)


def _split(a):
    hi = a.astype(BF16)
    lo = (a - hi.astype(F32)).astype(BF16)
    return hi, lo


class _Split:
    def __init__(self, a, exact=False):
        self.f32 = a
        if exact:
            self.hi, self.lo = a.astype(BF16), None
        else:
            self.hi, self.lo = _split(a)


def _dot3(a, b):
    out = _dot(a.hi, b.hi)
    if b.lo is not None:
        out = out + _dot(a.hi, b.lo)
    if a.lo is not None:
        out = out + _dot(a.lo, b.hi)
    return out


def _norm_mod(x, g, scale, shift):
    r = lax.rsqrt(jnp.mean(x * x, axis=-1, keepdims=True) + EPS)
    return (x * r) * g * (1.0 + scale) + shift


def _adaln_kernel(c_ref, w_ref, b_ref, o_ref):
    c_act = _silu(c_ref[...]).astype(BF16)
    o_ref[0] = _dot(c_act, w_ref[0].astype(BF16)) + b_ref[0]


def _adaln(c, ada_w, ada_b):
    depth, d, n = ada_w.shape
    bsz = c.shape[0]
    tn = 1536
    return pl.pallas_call(
        _adaln_kernel,
        grid=(depth, n // tn),
        in_specs=[
            pl.BlockSpec((bsz, d), lambda l, j: (0, 0)),
            pl.BlockSpec((1, d, tn), lambda l, j: (l, 0, j)),
            pl.BlockSpec((1, 1, tn), lambda l, j: (l, 0, j)),
        ],
        out_specs=pl.BlockSpec((1, bsz, tn), lambda l, j: (l, 0, j)),
        out_shape=jax.ShapeDtypeStruct((depth, bsz, n), F32),
        compiler_params=_cparams(("parallel", "parallel")),
        name="adaln_mod",
    )(c, ada_w, ada_b.reshape(depth, 1, n))


def _gdn_inproj_kernel(x_ref, g_ref, sc_ref, sh_ref, w_ref, wab_ref, o_ref, ab_ref):
    h = _norm_mod(x_ref[...], g_ref[...], sc_ref[0], sh_ref[0]).astype(BF16)
    ab_ref[...] = _dot(h, wab_ref[...])
    nout = o_ref.shape[1]
    for c0 in range(0, nout, GDN_INPROJ_COLS):
        cols = slice(c0, c0 + GDN_INPROJ_COLS)
        o_ref[:, cols] = _dot(h, w_ref[:, cols]).astype(o_ref.dtype)


def _gdn_inproj(x2, g, mod, w_main, w_ab, seq):
    n, d = x2.shape
    nout = w_main.shape[1]
    tm = ROW_TILE
    per_b = seq // tm
    return pl.pallas_call(
        _gdn_inproj_kernel,
        grid=(n // tm,),
        in_specs=[
            pl.BlockSpec((tm, d), lambda i: (i, 0)),
            pl.BlockSpec((1, d), lambda i: (0, 0)),
            pl.BlockSpec((1, 1, d), lambda i: ((i // per_b) * N_MOD + 1, 0, 0)),
            pl.BlockSpec((1, 1, d), lambda i: ((i // per_b) * N_MOD + 0, 0, 0)),
            pl.BlockSpec((d, nout), lambda i: (0, 0)),
            pl.BlockSpec((d, 128), lambda i: (0, 0)),
        ],
        out_specs=[
            pl.BlockSpec((tm, nout), lambda i: (i, 0)),
            pl.BlockSpec((tm, 128), lambda i: (i, 0)),
        ],
        out_shape=[
            jax.ShapeDtypeStruct((n, nout), BF16),
            jax.ShapeDtypeStruct((n, 128), F32),
        ],
        compiler_params=_cparams(("parallel",)),
        name="gdn_inproj",
    )(x2, g, mod, mod, w_main, w_ab)


def _run_interleaved(gens):
    while gens:
        alive = []
        for gen in gens:
            try:
                next(gen)
                alive.append(gen)
            except StopIteration:
                pass
        gens = alive


def _gdn_core_kernel(alog_ref, dtb_ref, q_ref, k_ref, v_ref, gate_ref, ab_ref, cw_ref, ng_ref,
                     o_ref, xc_ref, p_ref, ou_ref, m_ref, r_ref, egl_ref, tril_ref):
    hps = GDN_HEADS_PER_STEP
    hd = GDN_HEAD_DIM
    head0 = pl.program_id(1) * hps
    seq = q_ref.shape[1]
    cl = GDN_CHUNK
    n_chunks = seq // cl

    row = lax.broadcasted_iota(jnp.int32, (cl, cl), 0)
    col = lax.broadcasted_iota(jnp.int32, (cl, cl), 1)
    tril = row >= col
    strict = row > col
    tril_ref[...] = tril.astype(F32).astype(BF16)
    eye = (row == col).astype(F32)
    lane = lax.broadcasted_iota(jnp.int32, (cl, 128), 1)
    def same_block(size):
        shift = size.bit_length() - 1
        return jnp.right_shift(row, shift) == jnp.right_shift(col, shift)

    leaf_mask = same_block(GDN_INV_LEAF)
    merge_masks = {}
    size = GDN_INV_LEAF
    while size < cl:
        merge_masks[size] = same_block(2 * size) & jnp.logical_not(same_block(size))
        size *= 2

    conv_defs, prep_defs = [], []
    for hh in range(hps):
        head = head0 + hh
        lanes = slice(hh * hd, (hh + 1) * hd)

        def conv_slab(which, r0, delay, hh=hh, lanes=lanes):
            src_ref = (q_ref, k_ref, v_ref)[which]
            body = src_ref[0, pl.ds(r0, CONV_SLAB), lanes].astype(F32)
            h0 = pl.multiple_of(jnp.maximum(r0 - CONV_HALO, 0), CONV_HALO)
            halo = src_ref[0, pl.ds(h0, CONV_HALO), lanes].astype(F32)
            halo = jnp.where(r0 > 0, halo, 0.0)
            ext = jnp.concatenate([halo, body], axis=0)
            w = cw_ref[hh, which]
            for _ in range(1 + delay):
                yield
            acc = ext[CONV_HALO:] * w[GDN_CONV - 1:GDN_CONV]
            for sh in range(1, GDN_CONV):
                tap = GDN_CONV - 1 - sh
                acc = acc + pltpu.roll(ext, sh, 0)[CONV_HALO:] * w[tap:tap + 1]
            y = _silu(acc)
            if which < 2:
                y = y * lax.rsqrt(jnp.sum(y * y, axis=-1, keepdims=True) + EPS)
            if which == 0:
                y = y * (hd ** -0.5)
            yield
            xc_ref[hh % 2, which, pl.ds(r0, CONV_SLAB), :] = y

        conv_defs.append(conv_slab)

        a_head = jnp.exp(jnp.full((1, 1), alog_ref[head], F32))
        dt_head = dtb_ref[head]

        def prep_chunk(c, hh=hh, head=head, a_head=a_head, dt_head=dt_head):
            r0 = pl.multiple_of(c * cl, cl)
            q = xc_ref[hh % 2, 0, pl.ds(r0, cl), :]
            k = xc_ref[hh % 2, 1, pl.ds(r0, cl), :]
            v = xc_ref[hh % 2, 2, pl.ds(r0, cl), :]
            ab = ab_ref[0, pl.ds(r0, cl), :]
            yield
            a_col = jnp.sum(jnp.where(lane == head, ab, 0.0), axis=1, keepdims=True)
            b_col = jnp.sum(jnp.where(lane == head + GDN_HEADS, ab, 0.0), axis=1, keepdims=True)
            g = -a_head * _softplus(a_col + dt_head)
            beta = _sigmoid(b_col)
            g_hi, g_lo = _split(jnp.broadcast_to(g, (cl, 128)))
            tril_b = tril_ref[...]
            gc = _dot(tril_b, g_hi) + _dot(tril_b, g_lo)
            kb = k * beta
            k_b16 = k.astype(BF16)
            kk = _dot_nt(kb.astype(BF16), k_b16)
            qk = _dot_nt(q.astype(BF16), k_b16)
            yield
            gl = gc[cl - 1:cl, :]
            e_gc = jnp.exp(gc)
            e_rev = jnp.exp(gl - gc)
            e_gl = jnp.exp(gl)
            diff = gc[:, :cl] - gc[:, :cl].T
            dec = jnp.exp(jnp.where(tril, diff, -jnp.inf))
            lmat = jnp.where(strict, kk * dec, 0.0).astype(BF16).astype(F32)
            attn = (qk * dec).astype(BF16)
            rhs = jnp.concatenate([kb * e_gc, v * beta], axis=1).astype(BF16)
            kdt = (k * e_rev).T.astype(BF16)
            qd = q * e_gc
            nb = _Split(-jnp.where(leaf_mask, lmat, 0.0), exact=True)
            tinv = eye + nb.f32
            tinv_exact = True
            pw = nb
            span = 2
            while span < GDN_INV_LEAF:
                pw = _Split(_dot3(pw, pw))
                yield
                tinv = tinv + _dot3(_Split(tinv, exact=tinv_exact), pw)
                tinv_exact = False
                span *= 2
            size = GDN_INV_LEAF
            while size < cl:
                yield
                ts = _Split(tinv)
                xm = _dot3(_Split(jnp.where(merge_masks[size], lmat, 0.0), exact=True), ts)
                yield
                tinv = tinv - _dot3(ts, _Split(xm))
                size *= 2
            yield
            ts = _Split(tinv)
            wu = (_dot(ts.hi, rhs) + _dot(ts.lo, rhs)).astype(BF16)
            yield
            aw = _dot(attn, wu)
            kw = _dot(kdt, wu)
            yield
            p_ref[hh, c] = (qd - aw[:, :hd]).astype(BF16)
            ou_ref[hh, c] = aw[:, hd:]
            m_ref[hh, c] = (-kw[:, :hd]).astype(BF16)
            r_ref[hh, c] = kw[:, hd:]
            egl_ref[hh, c] = jnp.broadcast_to(e_gl, (8, 128))

        prep_defs.append(prep_chunk)

    groups = n_chunks // GDN_GROUP
    slabs_per_group = GDN_GROUP * cl // CONV_SLAB

    def conv_tasks(hh, i):
        tasks = []
        for t in range(slabs_per_group):
            r0 = pl.multiple_of((i * slabs_per_group + t) * CONV_SLAB, CONV_SLAB)
            for which in range(3):
                tasks.append(conv_defs[hh](which, r0, len(tasks)))
        return tasks

    def conv_only(i, carry):
        _run_interleaved(conv_tasks(0, i))
        return carry

    lax.fori_loop(0, groups, conv_only, 0)

    for hh in range(hps):
        def stage_body(i, carry, hh=hh):
            tasks = [prep_defs[hh](i * GDN_GROUP + u) for u in range(GDN_GROUP)]
            if hh + 1 < hps:
                tasks += conv_tasks(hh + 1, i)
            _run_interleaved(tasks)
            return carry

        lax.fori_loop(0, groups, stage_body, 0)

    ng = ng_ref[...]

    def scan_body(c, states):
        r0 = pl.multiple_of(c * cl, cl)
        new_states = [None] * hps

        def head_step(hh):
            s = states[hh]
            p = p_ref[hh, c]
            m = m_ref[hh, c]
            ou = ou_ref[hh, c]
            r = r_ref[hh, c]
            eg = egl_ref[hh, c][0:1, :]
            gt = gate_ref[0, pl.ds(r0, cl), hh * hd:(hh + 1) * hd].astype(F32)
            yield
            sb = s.astype(BF16)
            o = _dot(p, sb) + ou
            new_states[hh] = eg * s + (_dot(m, sb) + r)
            rn = lax.rsqrt(jnp.mean(o * o, axis=-1, keepdims=True) + EPS)
            y = ((o * rn) * ng * _silu(gt)).astype(o_ref.dtype)
            yield
            o_ref[0, pl.ds(r0, cl), hh * hd:(hh + 1) * hd] = y

        _run_interleaved([head_step(hh) for hh in range(hps)])
        return tuple(new_states)

    init = tuple(jnp.zeros((hd, hd), F32) for _ in range(hps))
    lax.fori_loop(0, n_chunks, scan_body, init)


def _gdn_core(main, ab, cw, a_log, dt_bias, norm_g, bsz, seq):
    hd = GDN_HEAD_DIM
    nh = GDN_HEADS
    hps = GDN_HEADS_PER_STEP
    groups = nh // hps
    n_chunks = seq // GDN_CHUNK
    main3 = main.reshape(bsz, seq, main.shape[1])
    ab3 = ab.reshape(bsz, seq, 128)
    smem = pl.BlockSpec(memory_space=pltpu.SMEM)

    def col_spec(which):
        return pl.BlockSpec((1, seq, hps * hd), lambda b, h, which=which: (b, 0, which * groups + h))

    return pl.pallas_call(
        _gdn_core_kernel,
        grid=(bsz, groups),
        in_specs=[
            smem, smem,
            col_spec(0), col_spec(1), col_spec(2), col_spec(3),
            pl.BlockSpec((1, seq, 128), lambda b, h: (b, 0, 0)),
            pl.BlockSpec((hps, 3, GDN_CONV, hd), lambda b, h: (h, 0, 0, 0)),
            pl.BlockSpec((1, hd), lambda b, h: (0, 0)),
        ],
        out_specs=pl.BlockSpec((1, seq, hps * hd), lambda b, h: (b, 0, h)),
        out_shape=jax.ShapeDtypeStruct((bsz, seq, nh * hd), BF16),
        scratch_shapes=[
            pltpu.VMEM((2, 3, seq, hd), F32),
            pltpu.VMEM((hps, n_chunks, GDN_CHUNK, hd), BF16),
            pltpu.VMEM((hps, n_chunks, GDN_CHUNK, hd), F32),
            pltpu.VMEM((hps, n_chunks, hd, hd), BF16),
            pltpu.VMEM((hps, n_chunks, hd, hd), F32),
            pltpu.VMEM((hps, n_chunks, 8, 128), F32),
            pltpu.VMEM((GDN_CHUNK, GDN_CHUNK), BF16),
        ],
        compiler_params=_cparams(("parallel", "parallel")),
        name="gdn_core",
    )(a_log, dt_bias, main3, main3, main3, main3, ab3, cw, norm_g)


def _mla_proj_kernel(x_ref, g_ref, sc_ref, sh_ref, pos_ref, tab_ref, win_ref, qg_ref, kvg_ref,
                     wq_ref, wqs_ref, wkn_ref, wv_ref, q_ref, kn_ref, kr_ref, v_ref):
    h = _norm_mod(x_ref[...], g_ref[...], sc_ref[0], sh_ref[0]).astype(BF16)
    proj = _dot(h, win_ref[...])
    qr, kvr = MLA_Q_RANK, MLA_KV_RANK
    c_q = proj[:, :qr]
    c_kv = proj[:, qr:qr + kvr]
    kr_a = proj[:, qr + kvr:qr + kvr + 128]
    kr_b = proj[:, qr + kvr + 128:qr + kvr + 256]

    def rms(t, gain):
        return (t * lax.rsqrt(jnp.mean(t * t, axis=-1, keepdims=True) + EPS)) * gain

    cqn = rms(c_q, qg_ref[...])
    ckvn = rms(c_kv, kvg_ref[...])
    cqn_t = cqn.T.astype(BF16)
    ckvn_t = ckvn.T.astype(BF16)
    ckvn = ckvn.astype(BF16)

    ang = pos_ref[...].astype(F32) * tab_ref[0:1, :]
    ct = jnp.cos(ang) * tab_ref[1:2, :]
    st = jnp.sin(ang) * tab_ref[2:3, :]
    kr_ref[...] = (kr_a * ct + kr_b * st).astype(kr_ref.dtype)
    ct_t = ct.T[:MLA_ROPE, :]
    st_t = st.T[:MLA_ROPE, :]

    scale = (MLA_NOPE + MLA_ROPE) ** -0.5 * LOG2_E
    qf = _dot(wq_ref[...], cqn_t)
    qs = _dot(wqs_ref[...], cqn_t)
    for hh in range(MLA_HEADS):
        base = hh * MLA_QK_PAD
        rope0 = base + MLA_NOPE
        q_ref[0, base:rope0, :] = (qf[base:rope0, :] * scale).astype(q_ref.dtype)
        rot = qf[rope0:rope0 + MLA_ROPE, :] * ct_t + qs[hh * MLA_ROPE:(hh + 1) * MLA_ROPE, :] * st_t
        q_ref[0, rope0:rope0 + MLA_ROPE, :] = (rot * scale).astype(q_ref.dtype)
        q_ref[0, rope0 + MLA_ROPE:base + MLA_QK_PAD, :] = jnp.zeros(
            (MLA_QK_PAD - MLA_NOPE - MLA_ROPE, q_ref.shape[2]), q_ref.dtype)

    kn_ref[...] = _dot(ckvn, wkn_ref[...]).astype(kn_ref.dtype)
    v_ref[0] = _dot(wv_ref[...], ckvn_t).astype(v_ref.dtype)


def _mla_proj(x2, g, mod, pos2, tab, w_in, qg, kvg, wq, wqs, wkn, wv, seq):
    n, d = x2.shape
    tm = ROW_TILE
    per_b = seq // tm
    full = lambda a: pl.BlockSpec(a.shape, lambda i: (0,) * a.ndim)
    hq = MLA_HEADS * MLA_QK_PAD
    hv = MLA_HEADS * MLA_V
    return pl.pallas_call(
        _mla_proj_kernel,
        grid=(n // tm,),
        in_specs=[
            pl.BlockSpec((tm, d), lambda i: (i, 0)),
            full(g),
            pl.BlockSpec((1, 1, d), lambda i: ((i // per_b) * N_MOD + 1, 0, 0)),
            pl.BlockSpec((1, 1, d), lambda i: ((i // per_b) * N_MOD + 0, 0, 0)),
            pl.BlockSpec((tm, 1), lambda i: (i, 0)),
            full(tab), full(w_in), full(qg), full(kvg), full(wq), full(wqs), full(wkn), full(wv),
        ],
        out_specs=[
            pl.BlockSpec((1, hq, tm), lambda i: (i // per_b, 0, i % per_b)),
            pl.BlockSpec((tm, hv), lambda i: (i, 0)),
            pl.BlockSpec((tm, 128), lambda i: (i, 0)),
            pl.BlockSpec((1, hv, tm), lambda i: (i // per_b, 0, i % per_b)),
        ],
        out_shape=[
            jax.ShapeDtypeStruct((n // seq, hq, seq), BF16),
            jax.ShapeDtypeStruct((n, hv), BF16),
            jax.ShapeDtypeStruct((n, 128), BF16),
            jax.ShapeDtypeStruct((n // seq, hv, seq), BF16),
        ],
        compiler_params=_cparams(("parallel",)),
        name="mla_proj",
    )(x2, g, mod, mod, pos2, tab, w_in, qg, kvg, wq, wqs, wkn, wv)


def _flash_kernel(q_ref, kn_ref, kr_ref, v_ref, o_ref):
    blk = ATTN_BLOCK
    n_blocks = kn_ref.shape[1] // blk

    def attend(qi):
        n_keys = qi + 1
        q_t = q_ref[0, :, qi * blk:(qi + 1) * blk]
        m = jnp.full((1, blk), -jnp.inf, F32)
        l = jnp.zeros((1, blk), F32)
        acc = jnp.zeros((MLA_V, blk), F32)
        kb = ATTN_KEY_BLOCK
        n_sub = n_keys * (blk // kb)
        diag0 = (n_keys - 1) * blk

        def logits(j):
            ks = slice(j * kb, (j + 1) * kb)
            kcat = jnp.concatenate([kn_ref[0, ks, :], kr_ref[0, ks, :]], axis=1)
            return _dot(kcat, q_t)

        s_next = logits(0)
        for j in range(n_sub):
            s = s_next
            if j + 1 < n_sub:
                s_next = logits(j + 1)
            if (j + 1) * kb > diag0:
                key = lax.broadcasted_iota(jnp.int32, s.shape, 0) + (j * kb - diag0)
                qry = lax.broadcasted_iota(jnp.int32, s.shape, 1)
                s = jnp.where(key <= qry, s, -jnp.inf)
            m_new = jnp.maximum(m, jnp.max(s, axis=0, keepdims=True))
            alpha = jnp.exp2(m - m_new)
            p = jnp.exp2(s - m_new)
            l = alpha * l + jnp.sum(p, axis=0, keepdims=True)
            acc = alpha * acc + _dot(v_ref[0, :, j * kb:(j + 1) * kb], p.astype(BF16))
            m = m_new
        o_ref[0, qi * blk:(qi + 1) * blk, :] = (acc / l).T.astype(o_ref.dtype)

    for i in range(n_blocks):
        attend(i)


def _flash_attention(q_t, kn, kr, v_t, bsz, seq):
    nh = MLA_HEADS
    blk = ATTN_BLOCK
    kn3 = kn.reshape(bsz, seq, nh * MLA_NOPE)
    kr3 = kr.reshape(bsz, seq, 128)
    return pl.pallas_call(
        _flash_kernel,
        grid=(bsz, nh),
        in_specs=[
            pl.BlockSpec((1, MLA_QK_PAD, seq), lambda b, h: (b, h, 0)),
            pl.BlockSpec((1, seq, MLA_NOPE), lambda b, h: (b, 0, h)),
            pl.BlockSpec((1, seq, 128), lambda b, h: (b, 0, 0)),
            pl.BlockSpec((1, MLA_V, seq), lambda b, h: (b, h, 0)),
        ],
        out_specs=pl.BlockSpec((1, seq, MLA_V), lambda b, h: (b, 0, h)),
        out_shape=jax.ShapeDtypeStruct((bsz, seq, nh * MLA_V), BF16),
        compiler_params=_cparams(("parallel", "parallel")),
        name="mla_flash",
    )(q_t, kn3, kr3, v_t)


def _ffn_kernel(a_ref, wo_ref, x_ref, gm_ref, g_ref, sc_ref, sh_ref, gt_ref, wg_ref, wu_ref, wd_ref,
                fg_ref, o_ref, *, final):
    xm = x_ref[...] + gm_ref[0] * _dot(a_ref[...], wo_ref[...])
    h = _norm_mod(xm, g_ref[...], sc_ref[0], sh_ref[0]).astype(BF16)
    dff = wg_ref.shape[1]
    acc = jnp.zeros(xm.shape, F32)
    for f0 in range(0, dff, dff // FFN_SPLIT):
        cols = slice(f0, f0 + dff // FFN_SPLIT)
        act = (_silu(_dot(h, wg_ref[:, cols])) * _dot(h, wu_ref[:, cols])).astype(BF16)
        acc = acc + _dot(act, wd_ref[cols, :])
    xn = xm + gt_ref[0] * acc
    if final:
        xn = (xn * lax.rsqrt(jnp.mean(xn * xn, axis=-1, keepdims=True) + EPS)) * fg_ref[...]
    o_ref[...] = xn


def _mixer_out_ffn(a2, w_out, x2, g, mod, wg, wu, wd, layer, fg, seq, final):
    n, d = x2.shape
    tm = ROW_TILE
    per_b = seq // tm
    mod_spec = lambda k: pl.BlockSpec((1, 1, d), lambda i, k=k: ((i // per_b) * N_MOD + k, 0, 0))
    resident = lambda a: pl.BlockSpec(a.shape, lambda i: (0,) * a.ndim, pipeline_mode=pl.Buffered(1))
    layer_resident = lambda a: pl.BlockSpec((None,) + a.shape[1:], lambda i: (layer, 0, 0),
                                            pipeline_mode=pl.Buffered(1))
    return pl.pallas_call(
        functools.partial(_ffn_kernel, final=final),
        grid=(n // tm,),
        in_specs=[
            pl.BlockSpec((tm, a2.shape[1]), lambda i: (i, 0)),
            resident(w_out),
            pl.BlockSpec((tm, d), lambda i: (i, 0)),
            mod_spec(2),
            pl.BlockSpec((1, d), lambda i: (0, 0)),
            mod_spec(4), mod_spec(3), mod_spec(5),
            layer_resident(wg), layer_resident(wu), layer_resident(wd),
            pl.BlockSpec((1, d), lambda i: (0, 0)),
        ],
        out_specs=pl.BlockSpec((tm, d), lambda i: (i, 0)),
        out_shape=jax.ShapeDtypeStruct((n, d), F32),
        compiler_params=_cparams(("parallel",)),
        name="ffn_final" if final else "ffn",
    )(a2, w_out, x2, mod, g, mod, mod, mod, wg, wu, wd, fg)


def _gdn_weights(w_in, conv_w):
    main_cols = 2 * GDN_HEADS * GDN_HEAD_DIM + 2 * GDN_HEADS * GDN_HEAD_DIM
    w_main = w_in[:, :main_cols].astype(BF16)
    w_ab = jnp.pad(w_in[:, main_cols:], ((0, 0), (0, 128 - 2 * GDN_HEADS))).astype(BF16)
    cw = conv_w.T.reshape(GDN_CONV, 3, GDN_HEADS, GDN_HEAD_DIM).transpose(2, 1, 0, 3)
    return w_main, w_ab, cw


def _mla_weights(w_in, w_uq, w_ukv):
    qr, kvr, half = MLA_Q_RANK, MLA_KV_RANK, MLA_ROPE // 2
    d = w_in.shape[0]
    rope = w_in[:, qr + kvr:]
    z = jnp.zeros((d, 128 - MLA_ROPE), w_in.dtype)
    w_in_ext = jnp.concatenate(
        [w_in[:, :qr + kvr], rope, z, rope[:, half:], rope[:, :half], z], axis=1).astype(BF16)
    uq = w_uq.reshape(qr, MLA_HEADS, MLA_NOPE + MLA_ROPE)
    nope, r = uq[..., :MLA_NOPE], uq[..., MLA_NOPE:]
    zq = jnp.zeros((qr, MLA_HEADS, 128 - MLA_ROPE), w_uq.dtype)
    wq = jnp.concatenate([nope, r, zq], axis=-1).reshape(qr, MLA_HEADS * MLA_QK_PAD).T.astype(BF16)
    wqs = jnp.concatenate([r[..., half:], r[..., :half]], axis=-1).reshape(
        qr, MLA_HEADS * MLA_ROPE).T.astype(BF16)
    ukv = w_ukv.reshape(kvr, MLA_HEADS, MLA_NOPE + MLA_V)
    wkn = ukv[..., :MLA_NOPE].reshape(kvr, MLA_HEADS * MLA_NOPE).astype(BF16)
    wv = ukv[..., MLA_NOPE:].reshape(kvr, MLA_HEADS * MLA_V).T.astype(BF16)
    return w_in_ext, wq, wqs, wkn, wv


def _rope_table():
    half = MLA_ROPE // 2
    inv_freq = ROPE_THETA ** (-jnp.arange(0, MLA_ROPE, 2, dtype=F32) / MLA_ROPE)
    z = jnp.zeros((128 - MLA_ROPE,), F32)
    ones = jnp.ones((half,), F32)
    freq = jnp.concatenate([inv_freq, inv_freq, z])
    cmask = jnp.concatenate([ones, ones, z])
    ssign = jnp.concatenate([-ones, ones, z])
    pad = jnp.zeros((5, 128), F32)
    return jnp.concatenate([jnp.stack([freq, cmask, ssign]), pad], axis=0)


def kernel(x, c, positions, ada_w, ada_b, norm_mix_g, norm_ffn_g, gdn_w_in, gdn_conv_w, gdn_a_log,
           gdn_dt_bias, gdn_norm_g, gdn_w_out, mla_w_in, mla_q_norm_g, mla_kv_norm_g, mla_w_uq,
           mla_w_ukv, mla_w_out, ffn_w_gate, ffn_w_up, ffn_w_down, final_norm_g):
    bsz, seq, d = x.shape
    depth = ada_w.shape[0]
    n_mixers = 2
    x2 = x.reshape(bsz * seq, d)
    pos2 = positions.reshape(bsz * seq, 1)
    tab = _rope_table()

    mod_all = _adaln(c, ada_w, ada_b).reshape(depth, bsz * N_MOD, 1, d)
    ffn_wg, ffn_wu, ffn_wd = (w.astype(BF16) for w in (ffn_w_gate, ffn_w_up, ffn_w_down))

    for layer in range(depth):
        mod = mod_all[layer]
        j = layer // n_mixers
        g_mix = norm_mix_g[layer].reshape(1, d)
        if layer % n_mixers == 0:
            w_main, w_ab, cw = _gdn_weights(gdn_w_in[j], gdn_conv_w[j])
            main, ab = _gdn_inproj(x2, g_mix, mod, w_main, w_ab, seq)
            o = _gdn_core(main, ab, cw, gdn_a_log[j], gdn_dt_bias[j],
                          gdn_norm_g[j].reshape(1, GDN_HEAD_DIM), bsz, seq)
            w_out = gdn_w_out[j].astype(BF16)
        else:
            w_in_ext, wq, wqs, wkn, wv = _mla_weights(mla_w_in[j], mla_w_uq[j], mla_w_ukv[j])
            qcat, kn, kr, v = _mla_proj(
                x2, g_mix, mod, pos2, tab, w_in_ext,
                mla_q_norm_g[j].reshape(1, -1), mla_kv_norm_g[j].reshape(1, -1),
                wq, wqs, wkn, wv, seq)
            o = _flash_attention(qcat, kn, kr, v, bsz, seq)
            w_out = mla_w_out[j].astype(BF16)

        x2 = _mixer_out_ffn(o.reshape(bsz * seq, -1), w_out, x2, norm_ffn_g[layer].reshape(1, d), mod,
                            ffn_wg, ffn_wu, ffn_wd, layer, final_norm_g.reshape(1, d), seq,
                            final=(layer == depth - 1))

    return x2.reshape(bsz, seq, d)
```

```python
import functools

import jax
import jax.numpy as jnp
from jax import lax
from jax.experimental import pallas as pl
from jax.experimental.pallas import tpu as pltpu

F32 = jnp.float32
BF16 = jnp.bfloat16

D_MODEL = 1024
N_MOD = 6
EPS = 1e-6

GDN_HEADS = 8
GDN_HEAD_DIM = 128
GDN_CONV = 4
GDN_CHUNK = 64
GDN_INPROJ_COLS = 1024
GDN_INV_LEAF = 8
GDN_HEADS_PER_STEP = 4
GDN_GROUP = 16
CONV_SLAB = 256
CONV_HALO = 16

MLA_HEADS = 8
MLA_NOPE = 128
MLA_ROPE = 64
MLA_V = 128
MLA_Q_RANK = 384
MLA_KV_RANK = 256
MLA_QK_PAD = 256
ROPE_THETA = 10000.0

ROW_TILE = 512
FFN_SPLIT = 1
ATTN_BLOCK = 512
ATTN_KEY_BLOCK = 256
LOG2_E = 1.4426950408889634
VMEM_LIMIT = 56 * 1024 * 1024


def _cparams(sem, vmem=VMEM_LIMIT):
    return pltpu.CompilerParams(dimension_semantics=sem, vmem_limit_bytes=vmem)


def _sigmoid(x):
    return 1.0 / (1.0 + jnp.exp(-x))


def _silu(x):
    return x * _sigmoid(x)


def _softplus(x):
    return jnp.maximum(x, 0.0) + jnp.log(1.0 + jnp.exp(-jnp.abs(x)))


def _dot(a, b):
    return jnp.dot(a, b, preferred_element_type=F32)


def _dot_nt(a, b):
    return lax.dot_general(a, b, (((1,), (1,)), ((), ())), preferred_element_type=F32)


def _split(a):
    hi = a.astype(BF16)
    lo = (a - hi.astype(F32)).astype(BF16)
    return hi, lo


def _norm_mod(x, g, scale, shift):
    r = lax.rsqrt(jnp.mean(x * x, axis=-1, keepdims=True) + EPS)
    return (x * r) * g * (1.0 + scale) + shift


def _adaln_kernel(c_ref, w_ref, b_ref, o_ref):
    c_act = _silu(c_ref[...]).astype(BF16)
    o_ref[0] = _dot(c_act, w_ref[0].astype(BF16)) + b_ref[0]


def _adaln(c, ada_w, ada_b):
    depth, d, n = ada_w.shape
    bsz = c.shape[0]
    tn = 1536
    return pl.pallas_call(
        _adaln_kernel,
        grid=(depth, n // tn),
        in_specs=[
            pl.BlockSpec((bsz, d), lambda l, j: (0, 0)),
            pl.BlockSpec((1, d, tn), lambda l, j: (l, 0, j)),
            pl.BlockSpec((1, 1, tn), lambda l, j: (l, 0, j)),
        ],
        out_specs=pl.BlockSpec((1, bsz, tn), lambda l, j: (l, 0, j)),
        out_shape=jax.ShapeDtypeStruct((depth, bsz, n), F32),
        compiler_params=_cparams(("parallel", "parallel")),
        name="adaln_mod",
    )(c, ada_w, ada_b.reshape(depth, 1, n))


def _gdn_inproj_kernel(x_ref, g_ref, sc_ref, sh_ref, w_ref, wab_ref, o_ref, ab_ref):
    h = _norm_mod(x_ref[...], g_ref[...], sc_ref[0], sh_ref[0]).astype(BF16)
    ab_ref[...] = _dot(h, wab_ref[...])
    nout = o_ref.shape[1]
    for c0 in range(0, nout, GDN_INPROJ_COLS):
        cols = slice(c0, c0 + GDN_INPROJ_COLS)
        o_ref[:, cols] = _dot(h, w_ref[:, cols]).astype(o_ref.dtype)


def _gdn_inproj(x2, g, mod, w_main, w_ab, seq):
    n, d = x2.shape
    nout = w_main.shape[1]
    tm = ROW_TILE
    per_b = seq // tm
    return pl.pallas_call(
        _gdn_inproj_kernel,
        grid=(n // tm,),
        in_specs=[
            pl.BlockSpec((tm, d), lambda i: (i, 0)),
            pl.BlockSpec((1, d), lambda i: (0, 0)),
            pl.BlockSpec((1, 1, d), lambda i: ((i // per_b) * N_MOD + 1, 0, 0)),
            pl.BlockSpec((1, 1, d), lambda i: ((i // per_b) * N_MOD + 0, 0, 0)),
            pl.BlockSpec((d, nout), lambda i: (0, 0)),
            pl.BlockSpec((d, 128), lambda i: (0, 0)),
        ],
        out_specs=[
            pl.BlockSpec((tm, nout), lambda i: (i, 0)),
            pl.BlockSpec((tm, 128), lambda i: (i, 0)),
        ],
        out_shape=[
            jax.ShapeDtypeStruct((n, nout), BF16),
            jax.ShapeDtypeStruct((n, 128), F32),
        ],
        compiler_params=_cparams(("parallel",)),
        name="gdn_inproj",
    )(x2, g, mod, mod, w_main, w_ab)


def _run_interleaved(gens):
    while gens:
        alive = []
        for gen in gens:
            try:
                next(gen)
                alive.append(gen)
            except StopIteration:
                pass
        gens = alive


def _gdn_core_kernel(alog_ref, dtb_ref, q_ref, k_ref, v_ref, gate_ref, ab_ref, cw_ref, ng_ref,
                     o_ref, xc_ref, p_ref, ou_ref, m_ref, r_ref, egl_ref, tril_ref):
    hps = GDN_HEADS_PER_STEP
    hd = GDN_HEAD_DIM
    head0 = pl.program_id(1) * hps
    seq = q_ref.shape[1]
    cl = GDN_CHUNK
    n_chunks = seq // cl

    row = lax.broadcasted_iota(jnp.int32, (cl, cl), 0)
    col = lax.broadcasted_iota(jnp.int32, (cl, cl), 1)
    tril = row >= col
    strict = row > col
    tril_ref[...] = tril.astype(F32).astype(BF16)
    eye = (row == col).astype(F32)
    lane = lax.broadcasted_iota(jnp.int32, (cl, 128), 1)
    def same_block(size):
        shift = size.bit_length() - 1
        return jnp.right_shift(row, shift) == jnp.right_shift(col, shift)

    leaf_mask = same_block(GDN_INV_LEAF)
    merge_masks = {}
    size = GDN_INV_LEAF
    while size < cl:
        merge_masks[size] = same_block(2 * size) & jnp.logical_not(same_block(size))
        size *= 2

    conv_defs, prep_defs = [], []
    for hh in range(hps):
        head = head0 + hh
        lanes = slice(hh * hd, (hh + 1) * hd)

        def conv_slab(which, r0, delay, hh=hh, lanes=lanes):
            src_ref = (q_ref, k_ref, v_ref)[which]
            body = src_ref[0, pl.ds(r0, CONV_SLAB), lanes].astype(F32)
            h0 = pl.multiple_of(jnp.maximum(r0 - CONV_HALO, 0), CONV_HALO)
            halo = src_ref[0, pl.ds(h0, CONV_HALO), lanes].astype(F32)
            halo = jnp.where(r0 > 0, halo, 0.0)
            ext = jnp.concatenate([halo, body], axis=0)
            w = cw_ref[hh, which]
            for _ in range(1 + delay):
                yield
            acc = ext[CONV_HALO:] * w[GDN_CONV - 1:GDN_CONV]
            for sh in range(1, GDN_CONV):
                tap = GDN_CONV - 1 - sh
                acc = acc + pltpu.roll(ext, sh, 0)[CONV_HALO:] * w[tap:tap + 1]
            y = _silu(acc)
            if which < 2:
                y = y * lax.rsqrt(jnp.sum(y * y, axis=-1, keepdims=True) + EPS)
            if which == 0:
                y = y * (hd ** -0.5)
            yield
            xc_ref[hh % 2, which, pl.ds(r0, CONV_SLAB), :] = y

        conv_defs.append(conv_slab)

        a_head = jnp.exp(jnp.full((1, 1), alog_ref[head], F32))
        dt_head = dtb_ref[head]

        def prep_chunk(c, hh=hh, head=head, a_head=a_head, dt_head=dt_head):
            r0 = pl.multiple_of(c * cl, cl)
            q = xc_ref[hh % 2, 0, pl.ds(r0, cl), :]
            k = xc_ref[hh % 2, 1, pl.ds(r0, cl), :]
            v = xc_ref[hh % 2, 2, pl.ds(r0, cl), :]
            ab = ab_ref[0, pl.ds(r0, cl), :]
            yield
            a_col = jnp.sum(jnp.where(lane == head, ab, 0.0), axis=1, keepdims=True)
            b_col = jnp.sum(jnp.where(lane == head + GDN_HEADS, ab, 0.0), axis=1, keepdims=True)
            g = -a_head * _softplus(a_col + dt_head)
            beta = _sigmoid(b_col)
            g_hi, g_lo = _split(jnp.broadcast_to(g, (cl, 128)))
            tril_b = tril_ref[...]
            gc = _dot(tril_b, g_hi) + _dot(tril_b, g_lo)
            kb = k * beta
            k_b16 = k.astype(BF16)
            kk = _dot_nt(kb.astype(BF16), k_b16)
            qk = _dot_nt(q.astype(BF16), k_b16)
            yield
            gl = gc[cl - 1:cl, :]
            e_gc = jnp.exp(gc)
            e_rev = jnp.exp(gl - gc)
            e_gl = jnp.exp(gl)
            diff = gc[:, :cl] - gc[:, :cl].T
            dec = jnp.exp(jnp.where(tril, diff, -jnp.inf))
            lmat = jnp.where(strict, kk * dec, 0.0).astype(BF16).astype(F32)
            attn = (qk * dec).astype(BF16)
            rhs = jnp.concatenate([kb * e_gc, v * beta], axis=1).astype(BF16)
            kdt = (k * e_rev).T.astype(BF16)
            qd = q * e_gc
            pw = -jnp.where(leaf_mask, lmat, 0.0)
            tinv = eye + pw
            span = 2
            while span < GDN_INV_LEAF:
                pw_b = pw.astype(BF16)
                pw = _dot(pw_b, pw_b)
                yield
                tinv = tinv + _dot(tinv.astype(BF16), pw.astype(BF16))
                span *= 2
            size = GDN_INV_LEAF
            while size < cl:
                yield
                t_b = tinv.astype(BF16)
                xm = _dot(jnp.where(merge_masks[size], lmat, 0.0).astype(BF16), t_b)
                yield
                tinv = tinv - _dot(t_b, xm.astype(BF16))
                size *= 2
            yield
            wu = _dot(tinv.astype(BF16), rhs).astype(BF16)
            yield
            aw = _dot(attn, wu)
            kw = _dot(kdt, wu)
            yield
            p_ref[hh, c] = (qd - aw[:, :hd]).astype(BF16)
            ou_ref[hh, c] = aw[:, hd:]
            m_ref[hh, c] = (-kw[:, :hd]).astype(BF16)
            r_ref[hh, c] = kw[:, hd:]
            egl_ref[hh, c] = jnp.broadcast_to(e_gl, (8, 128))

        prep_defs.append(prep_chunk)

    groups = n_chunks // GDN_GROUP
    slabs_per_group = GDN_GROUP * cl // CONV_SLAB

    def conv_tasks(hh, i):
        tasks = []
        for t in range(slabs_per_group):
            r0 = pl.multiple_of((i * slabs_per_group + t) * CONV_SLAB, CONV_SLAB)
            for which in range(3):
                tasks.append(conv_defs[hh](which, r0, len(tasks)))
        return tasks

    def conv_only(i, carry):
        _run_interleaved(conv_tasks(0, i))
        return carry

    lax.fori_loop(0, groups, conv_only, 0)

    for hh in range(hps):
        def stage_body(i, carry, hh=hh):
            tasks = [prep_defs[hh](i * GDN_GROUP + u) for u in range(GDN_GROUP)]
            if hh + 1 < hps:
                tasks += conv_tasks(hh + 1, i)
            _run_interleaved(tasks)
            return carry

        lax.fori_loop(0, groups, stage_body, 0)

    ng = ng_ref[...]

    def scan_body(c, states):
        r0 = pl.multiple_of(c * cl, cl)
        new_states = [None] * hps

        def head_step(hh):
            s = states[hh]
            p = p_ref[hh, c]
            m = m_ref[hh, c]
            ou = ou_ref[hh, c]
            r = r_ref[hh, c]
            eg = egl_ref[hh, c][0:1, :]
            gt = gate_ref[0, pl.ds(r0, cl), hh * hd:(hh + 1) * hd].astype(F32)
            yield
            sb = s.astype(BF16)
            o = _dot(p, sb) + ou
            new_states[hh] = eg * s + (_dot(m, sb) + r)
            rn = lax.rsqrt(jnp.mean(o * o, axis=-1, keepdims=True) + EPS)
            y = ((o * rn) * ng * _silu(gt)).astype(o_ref.dtype)
            yield
            o_ref[0, pl.ds(r0, cl), hh * hd:(hh + 1) * hd] = y

        _run_interleaved([head_step(hh) for hh in range(hps)])
        return tuple(new_states)

    init = tuple(jnp.zeros((hd, hd), F32) for _ in range(hps))
    lax.fori_loop(0, n_chunks, scan_body, init)


def _gdn_core(main, ab, cw, a_log, dt_bias, norm_g, bsz, seq):
    hd = GDN_HEAD_DIM
    nh = GDN_HEADS
    hps = GDN_HEADS_PER_STEP
    groups = nh // hps
    n_chunks = seq // GDN_CHUNK
    main3 = main.reshape(bsz, seq, main.shape[1])
    ab3 = ab.reshape(bsz, seq, 128)
    smem = pl.BlockSpec(memory_space=pltpu.SMEM)

    def col_spec(which):
        return pl.BlockSpec((1, seq, hps * hd), lambda b, h, which=which: (b, 0, which * groups + h))

    return pl.pallas_call(
        _gdn_core_kernel,
        grid=(bsz, groups),
        in_specs=[
            smem, smem,
            col_spec(0), col_spec(1), col_spec(2), col_spec(3),
            pl.BlockSpec((1, seq, 128), lambda b, h: (b, 0, 0)),
            pl.BlockSpec((hps, 3, GDN_CONV, hd), lambda b, h: (h, 0, 0, 0)),
            pl.BlockSpec((1, hd), lambda b, h: (0, 0)),
        ],
        out_specs=pl.BlockSpec((1, seq, hps * hd), lambda b, h: (b, 0, h)),
        out_shape=jax.ShapeDtypeStruct((bsz, seq, nh * hd), BF16),
        scratch_shapes=[
            pltpu.VMEM((2, 3, seq, hd), F32),
            pltpu.VMEM((hps, n_chunks, GDN_CHUNK, hd), BF16),
            pltpu.VMEM((hps, n_chunks, GDN_CHUNK, hd), F32),
            pltpu.VMEM((hps, n_chunks, hd, hd), BF16),
            pltpu.VMEM((hps, n_chunks, hd, hd), F32),
            pltpu.VMEM((hps, n_chunks, 8, 128), F32),
            pltpu.VMEM((GDN_CHUNK, GDN_CHUNK), BF16),
        ],
        compiler_params=_cparams(("parallel", "parallel")),
        name="gdn_core",
    )(a_log, dt_bias, main3, main3, main3, main3, ab3, cw, norm_g)


def _mla_proj_kernel(x_ref, g_ref, sc_ref, sh_ref, pos_ref, tab_ref, win_ref, qg_ref, kvg_ref,
                     wq_ref, wqs_ref, wkn_ref, wv_ref, q_ref, kn_ref, kr_ref, v_ref):
    h = _norm_mod(x_ref[...], g_ref[...], sc_ref[0], sh_ref[0]).astype(BF16)
    proj = _dot(h, win_ref[...])
    qr, kvr = MLA_Q_RANK, MLA_KV_RANK
    c_q = proj[:, :qr]
    c_kv = proj[:, qr:qr + kvr]
    kr_a = proj[:, qr + kvr:qr + kvr + 128]
    kr_b = proj[:, qr + kvr + 128:qr + kvr + 256]

    def rms(t, gain):
        return (t * lax.rsqrt(jnp.mean(t * t, axis=-1, keepdims=True) + EPS)) * gain

    cqn = rms(c_q, qg_ref[...])
    ckvn = rms(c_kv, kvg_ref[...])
    cqn_t = cqn.T.astype(BF16)
    ckvn_t = ckvn.T.astype(BF16)
    ckvn = ckvn.astype(BF16)

    ang = pos_ref[...].astype(F32) * tab_ref[0:1, :]
    ct = jnp.cos(ang) * tab_ref[1:2, :]
    st = jnp.sin(ang) * tab_ref[2:3, :]
    kr_ref[...] = (kr_a * ct + kr_b * st).astype(kr_ref.dtype)
    ct_t = ct.T[:MLA_ROPE, :]
    st_t = st.T[:MLA_ROPE, :]

    scale = (MLA_NOPE + MLA_ROPE) ** -0.5 * LOG2_E
    qf = _dot(wq_ref[...], cqn_t)
    qs = _dot(wqs_ref[...], cqn_t)
    for hh in range(MLA_HEADS):
        base = hh * MLA_QK_PAD
        rope0 = base + MLA_NOPE
        q_ref[0, base:rope0, :] = (qf[base:rope0, :] * scale).astype(q_ref.dtype)
        rot = qf[rope0:rope0 + MLA_ROPE, :] * ct_t + qs[hh * MLA_ROPE:(hh + 1) * MLA_ROPE, :] * st_t
        q_ref[0, rope0:rope0 + MLA_ROPE, :] = (rot * scale).astype(q_ref.dtype)
        q_ref[0, rope0 + MLA_ROPE:base + MLA_QK_PAD, :] = jnp.zeros(
            (MLA_QK_PAD - MLA_NOPE - MLA_ROPE, q_ref.shape[2]), q_ref.dtype)

    kn_ref[...] = _dot(ckvn, wkn_ref[...]).astype(kn_ref.dtype)
    v_ref[0] = _dot(wv_ref[...], ckvn_t).astype(v_ref.dtype)


def _mla_proj(x2, g, mod, pos2, tab, w_in, qg, kvg, wq, wqs, wkn, wv, seq):
    n, d = x2.shape
    tm = ROW_TILE
    per_b = seq // tm
    full = lambda a: pl.BlockSpec(a.shape, lambda i: (0,) * a.ndim)
    hq = MLA_HEADS * MLA_QK_PAD
    hv = MLA_HEADS * MLA_V
    return pl.pallas_call(
        _mla_proj_kernel,
        grid=(n // tm,),
        in_specs=[
            pl.BlockSpec((tm, d), lambda i: (i, 0)),
            full(g),
            pl.BlockSpec((1, 1, d), lambda i: ((i // per_b) * N_MOD + 1, 0, 0)),
            pl.BlockSpec((1, 1, d), lambda i: ((i // per_b) * N_MOD + 0, 0, 0)),
            pl.BlockSpec((tm, 1), lambda i: (i, 0)),
            full(tab), full(w_in), full(qg), full(kvg), full(wq), full(wqs), full(wkn), full(wv),
        ],
        out_specs=[
            pl.BlockSpec((1, hq, tm), lambda i: (i // per_b, 0, i % per_b)),
            pl.BlockSpec((tm, hv), lambda i: (i, 0)),
            pl.BlockSpec((tm, 128), lambda i: (i, 0)),
            pl.BlockSpec((1, hv, tm), lambda i: (i // per_b, 0, i % per_b)),
        ],
        out_shape=[
            jax.ShapeDtypeStruct((n // seq, hq, seq), BF16),
            jax.ShapeDtypeStruct((n, hv), BF16),
            jax.ShapeDtypeStruct((n, 128), BF16),
            jax.ShapeDtypeStruct((n // seq, hv, seq), BF16),
        ],
        compiler_params=_cparams(("parallel",)),
        name="mla_proj",
    )(x2, g, mod, mod, pos2, tab, w_in, qg, kvg, wq, wqs, wkn, wv)


def _flash_kernel(q_ref, kn_ref, kr_ref, v_ref, o_ref):
    blk = ATTN_BLOCK
    n_blocks = kn_ref.shape[1] // blk

    def attend(qi):
        n_keys = qi + 1
        q_t = q_ref[0, :, qi * blk:(qi + 1) * blk]
        m = jnp.full((1, blk), -jnp.inf, F32)
        l = jnp.zeros((1, blk), F32)
        acc = jnp.zeros((MLA_V, blk), F32)
        kb = ATTN_KEY_BLOCK
        n_sub = n_keys * (blk // kb)
        diag0 = (n_keys - 1) * blk

        def logits(j):
            ks = slice(j * kb, (j + 1) * kb)
            kcat = jnp.concatenate([kn_ref[0, ks, :], kr_ref[0, ks, :]], axis=1)
            return _dot(kcat, q_t)

        s_next = logits(0)
        for j in range(n_sub):
            s = s_next
            if j + 1 < n_sub:
                s_next = logits(j + 1)
            if (j + 1) * kb > diag0:
                key = lax.broadcasted_iota(jnp.int32, s.shape, 0) + (j * kb - diag0)
                qry = lax.broadcasted_iota(jnp.int32, s.shape, 1)
                s = jnp.where(key <= qry, s, -jnp.inf)
            m_new = jnp.maximum(m, jnp.max(s, axis=0, keepdims=True))
            alpha = jnp.exp2(m - m_new)
            p = jnp.exp2(s - m_new)
            l = alpha * l + jnp.sum(p, axis=0, keepdims=True)
            acc = alpha * acc + _dot(v_ref[0, :, j * kb:(j + 1) * kb], p.astype(BF16))
            m = m_new
        o_ref[0, qi * blk:(qi + 1) * blk, :] = (acc / l).T.astype(o_ref.dtype)

    for i in range(n_blocks):
        attend(i)


def _flash_attention(q_t, kn, kr, v_t, bsz, seq):
    nh = MLA_HEADS
    blk = ATTN_BLOCK
    kn3 = kn.reshape(bsz, seq, nh * MLA_NOPE)
    kr3 = kr.reshape(bsz, seq, 128)
    return pl.pallas_call(
        _flash_kernel,
        grid=(bsz, nh),
        in_specs=[
            pl.BlockSpec((1, MLA_QK_PAD, seq), lambda b, h: (b, h, 0)),
            pl.BlockSpec((1, seq, MLA_NOPE), lambda b, h: (b, 0, h)),
            pl.BlockSpec((1, seq, 128), lambda b, h: (b, 0, 0)),
            pl.BlockSpec((1, MLA_V, seq), lambda b, h: (b, h, 0)),
        ],
        out_specs=pl.BlockSpec((1, seq, MLA_V), lambda b, h: (b, 0, h)),
        out_shape=jax.ShapeDtypeStruct((bsz, seq, nh * MLA_V), BF16),
        compiler_params=_cparams(("parallel", "parallel")),
        name="mla_flash",
    )(q_t, kn3, kr3, v_t)


def _ffn_kernel(a_ref, wo_ref, x_ref, gm_ref, g_ref, sc_ref, sh_ref, gt_ref, wg_ref, wu_ref, wd_ref,
                fg_ref, o_ref, *, final):
    xm = x_ref[...] + gm_ref[0] * _dot(a_ref[...], wo_ref[...])
    h = _norm_mod(xm, g_ref[...], sc_ref[0], sh_ref[0]).astype(BF16)
    dff = wg_ref.shape[1]
    acc = jnp.zeros(xm.shape, F32)
    for f0 in range(0, dff, dff // FFN_SPLIT):
        cols = slice(f0, f0 + dff // FFN_SPLIT)
        act = (_silu(_dot(h, wg_ref[:, cols])) * _dot(h, wu_ref[:, cols])).astype(BF16)
        acc = acc + _dot(act, wd_ref[cols, :])
    xn = xm + gt_ref[0] * acc
    if final:
        xn = (xn * lax.rsqrt(jnp.mean(xn * xn, axis=-1, keepdims=True) + EPS)) * fg_ref[...]
    o_ref[...] = xn


def _mixer_out_ffn(a2, w_out, x2, g, mod, wg, wu, wd, layer, fg, seq, final):
    n, d = x2.shape
    tm = ROW_TILE
    per_b = seq // tm
    mod_spec = lambda k: pl.BlockSpec((1, 1, d), lambda i, k=k: ((i // per_b) * N_MOD + k, 0, 0))
    resident = lambda a: pl.BlockSpec(a.shape, lambda i: (0,) * a.ndim, pipeline_mode=pl.Buffered(1))
    layer_resident = lambda a: pl.BlockSpec((None,) + a.shape[1:], lambda i: (layer, 0, 0),
                                            pipeline_mode=pl.Buffered(1))
    return pl.pallas_call(
        functools.partial(_ffn_kernel, final=final),
        grid=(n // tm,),
        in_specs=[
            pl.BlockSpec((tm, a2.shape[1]), lambda i: (i, 0)),
            resident(w_out),
            pl.BlockSpec((tm, d), lambda i: (i, 0)),
            mod_spec(2),
            pl.BlockSpec((1, d), lambda i: (0, 0)),
            mod_spec(4), mod_spec(3), mod_spec(5),
            layer_resident(wg), layer_resident(wu), layer_resident(wd),
            pl.BlockSpec((1, d), lambda i: (0, 0)),
        ],
        out_specs=pl.BlockSpec((tm, d), lambda i: (i, 0)),
        out_shape=jax.ShapeDtypeStruct((n, d), F32),
        compiler_params=_cparams(("parallel",)),
        name="ffn_final" if final else "ffn",
    )(a2, w_out, x2, mod, g, mod, mod, mod, wg, wu, wd, fg)


def _gdn_weights(w_in, conv_w):
    main_cols = 2 * GDN_HEADS * GDN_HEAD_DIM + 2 * GDN_HEADS * GDN_HEAD_DIM
    w_main = w_in[:, :main_cols].astype(BF16)
    w_ab = jnp.pad(w_in[:, main_cols:], ((0, 0), (0, 128 - 2 * GDN_HEADS))).astype(BF16)
    cw = conv_w.T.reshape(GDN_CONV, 3, GDN_HEADS, GDN_HEAD_DIM).transpose(2, 1, 0, 3)
    return w_main, w_ab, cw


def _mla_weights(w_in, w_uq, w_ukv):
    qr, kvr, half = MLA_Q_RANK, MLA_KV_RANK, MLA_ROPE // 2
    d = w_in.shape[0]
    rope = w_in[:, qr + kvr:]
    z = jnp.zeros((d, 128 - MLA_ROPE), w_in.dtype)
    w_in_ext = jnp.concatenate(
        [w_in[:, :qr + kvr], rope, z, rope[:, half:], rope[:, :half], z], axis=1).astype(BF16)
    uq = w_uq.reshape(qr, MLA_HEADS, MLA_NOPE + MLA_ROPE)
    nope, r = uq[..., :MLA_NOPE], uq[..., MLA_NOPE:]
    zq = jnp.zeros((qr, MLA_HEADS, 128 - MLA_ROPE), w_uq.dtype)
    wq = jnp.concatenate([nope, r, zq], axis=-1).reshape(qr, MLA_HEADS * MLA_QK_PAD).T.astype(BF16)
    wqs = jnp.concatenate([r[..., half:], r[..., :half]], axis=-1).reshape(
        qr, MLA_HEADS * MLA_ROPE).T.astype(BF16)
    ukv = w_ukv.reshape(kvr, MLA_HEADS, MLA_NOPE + MLA_V)
    wkn = ukv[..., :MLA_NOPE].reshape(kvr, MLA_HEADS * MLA_NOPE).astype(BF16)
    wv = ukv[..., MLA_NOPE:].reshape(kvr, MLA_HEADS * MLA_V).T.astype(BF16)
    return w_in_ext, wq, wqs, wkn, wv


def _rope_table():
    half = MLA_ROPE // 2
    inv_freq = ROPE_THETA ** (-jnp.arange(0, MLA_ROPE, 2, dtype=F32) / MLA_ROPE)
    z = jnp.zeros((128 - MLA_ROPE,), F32)
    ones = jnp.ones((half,), F32)
    freq = jnp.concatenate([inv_freq, inv_freq, z])
    cmask = jnp.concatenate([ones, ones, z])
    ssign = jnp.concatenate([-ones, ones, z])
    pad = jnp.zeros((5, 128), F32)
    return jnp.concatenate([jnp.stack([freq, cmask, ssign]), pad], axis=0)


def kernel(x, c, positions, ada_w, ada_b, norm_mix_g, norm_ffn_g, gdn_w_in, gdn_conv_w, gdn_a_log,
           gdn_dt_bias, gdn_norm_g, gdn_w_out, mla_w_in, mla_q_norm_g, mla_kv_norm_g, mla_w_uq,
           mla_w_ukv, mla_w_out, ffn_w_gate, ffn_w_up, ffn_w_down, final_norm_g):
    bsz, seq, d = x.shape
    depth = ada_w.shape[0]
    n_mixers = 2
    x2 = x.reshape(bsz * seq, d)
    pos2 = positions.reshape(bsz * seq, 1)
    tab = _rope_table()

    mod_all = _adaln(c, ada_w, ada_b).reshape(depth, bsz * N_MOD, 1, d)
    ffn_wg, ffn_wu, ffn_wd = (w.astype(BF16) for w in (ffn_w_gate, ffn_w_up, ffn_w_down))

    for layer in range(depth):
        mod = mod_all[layer]
        j = layer // n_mixers
        g_mix = norm_mix_g[layer].reshape(1, d)
        if layer % n_mixers == 0:
            w_main, w_ab, cw = _gdn_weights(gdn_w_in[j], gdn_conv_w[j])
            main, ab = _gdn_inproj(x2, g_mix, mod, w_main, w_ab, seq)
            o = _gdn_core(main, ab, cw, gdn_a_log[j], gdn_dt_bias[j],
                          gdn_norm_g[j].reshape(1, GDN_HEAD_DIM), bsz, seq)
            w_out = gdn_w_out[j].astype(BF16)
        else:
            w_in_ext, wq, wqs, wkn, wv = _mla_weights(mla_w_in[j], mla_w_uq[j], mla_w_ukv[j])
            qcat, kn, kr, v = _mla_proj(
                x2, g_mix, mod, pos2, tab, w_in_ext,
                mla_q_norm_g[j].reshape(1, -1), mla_kv_norm_g[j].reshape(1, -1),
                wq, wqs, wkn, wv, seq)
            o = _flash_attention(qcat, kn, kr, v, bsz, seq)
            w_out = mla_w_out[j].astype(BF16)

        x2 = _mixer_out_ffn(o.reshape(bsz * seq, -1), w_out, x2, norm_ffn_g[layer].reshape(1, d), mod,
                            ffn_wg, ffn_wu, ffn_wd, layer, final_norm_g.reshape(1, d), seq,
                            final=(layer == depth - 1))

    return x2.reshape(bsz, seq, d)
```

```python
import functools

import jax
import jax.numpy as jnp
from jax import lax
from jax.experimental import pallas as pl
from jax.experimental.pallas import tpu as pltpu

F32 = jnp.float32
BF16 = jnp.bfloat16

D_MODEL = 1024
N_MOD = 6
EPS = 1e-6

GDN_HEADS = 8
GDN_HEAD_DIM = 128
GDN_CONV = 4
GDN_CHUNK = 64
GDN_INPROJ_COLS = 1024
GDN_INV_LEAF = 8
GDN_HEADS_PER_STEP = 4
GDN_GROUP = 16
CONV_SLAB = 256
CONV_HALO = 16

MLA_HEADS = 8
MLA_NOPE = 128
MLA_ROPE = 64
MLA_V = 128
MLA_Q_RANK = 384
MLA_KV_RANK = 256
MLA_QK_PAD = 256
ROPE_THETA = 10000.0

ROW_TILE = 512
FFN_SPLIT = 1
ATTN_BLOCK = 512
ATTN_KEY_BLOCK = 512
LOG2_E = 1.4426950408889634
VMEM_LIMIT = 56 * 1024 * 1024


def _cparams(sem, vmem=VMEM_LIMIT):
    return pltpu.CompilerParams(dimension_semantics=sem, vmem_limit_bytes=vmem)


def _sigmoid(x):
    return 1.0 / (1.0 + jnp.exp(-x))


def _silu(x):
    return x * _sigmoid(x)


def _softplus(x):
    return jnp.maximum(x, 0.0) + jnp.log(1.0 + jnp.exp(-jnp.abs(x)))


def _dot(a, b):
    return jnp.dot(a, b, preferred_element_type=F32)


def _dot_nt(a, b):
    return lax.dot_general(a, b, (((1,), (1,)), ((), ())), preferred_element_type=F32)


def _split(a):
    hi = a.astype(BF16)
    lo = (a - hi.astype(F32)).astype(BF16)
    return hi, lo


def _norm_mod(x, g, scale, shift):
    r = lax.rsqrt(jnp.mean(x * x, axis=-1, keepdims=True) + EPS)
    return (x * r) * g * (1.0 + scale) + shift


def _adaln_kernel(c_ref, w_ref, b_ref, o_ref):
    c_act = _silu(c_ref[...]).astype(BF16)
    o_ref[0] = _dot(c_act, w_ref[0].astype(BF16)) + b_ref[0]


def _adaln(c, ada_w, ada_b):
    depth, d, n = ada_w.shape
    bsz = c.shape[0]
    tn = 1536
    return pl.pallas_call(
        _adaln_kernel,
        grid=(depth, n // tn),
        in_specs=[
            pl.BlockSpec((bsz, d), lambda l, j: (0, 0)),
            pl.BlockSpec((1, d, tn), lambda l, j: (l, 0, j)),
            pl.BlockSpec((1, 1, tn), lambda l, j: (l, 0, j)),
        ],
        out_specs=pl.BlockSpec((1, bsz, tn), lambda l, j: (l, 0, j)),
        out_shape=jax.ShapeDtypeStruct((depth, bsz, n), F32),
        compiler_params=_cparams(("parallel", "parallel")),
        name="adaln_mod",
    )(c, ada_w, ada_b.reshape(depth, 1, n))


def _gdn_inproj_kernel(x_ref, g_ref, sc_ref, sh_ref, w_ref, wab_ref, o_ref, ab_ref):
    h = _norm_mod(x_ref[...], g_ref[...], sc_ref[0], sh_ref[0]).astype(BF16)
    ab_ref[...] = _dot(h, wab_ref[...])
    nout = o_ref.shape[1]
    for c0 in range(0, nout, GDN_INPROJ_COLS):
        cols = slice(c0, c0 + GDN_INPROJ_COLS)
        o_ref[:, cols] = _dot(h, w_ref[:, cols]).astype(o_ref.dtype)


def _gdn_inproj(x2, g, mod, w_main, w_ab, seq):
    n, d = x2.shape
    nout = w_main.shape[1]
    tm = ROW_TILE
    per_b = seq // tm
    return pl.pallas_call(
        _gdn_inproj_kernel,
        grid=(n // tm,),
        in_specs=[
            pl.BlockSpec((tm, d), lambda i: (i, 0)),
            pl.BlockSpec((1, d), lambda i: (0, 0)),
            pl.BlockSpec((1, 1, d), lambda i: ((i // per_b) * N_MOD + 1, 0, 0)),
            pl.BlockSpec((1, 1, d), lambda i: ((i // per_b) * N_MOD + 0, 0, 0)),
            pl.BlockSpec((d, nout), lambda i: (0, 0)),
            pl.BlockSpec((d, 128), lambda i: (0, 0)),
        ],
        out_specs=[
            pl.BlockSpec((tm, nout), lambda i: (i, 0)),
            pl.BlockSpec((tm, 128), lambda i: (i, 0)),
        ],
        out_shape=[
            jax.ShapeDtypeStruct((n, nout), BF16),
            jax.ShapeDtypeStruct((n, 128), F32),
        ],
        compiler_params=_cparams(("parallel",)),
        name="gdn_inproj",
    )(x2, g, mod, mod, w_main, w_ab)


def _run_interleaved(gens):
    while gens:
        alive = []
        for gen in gens:
            try:
                next(gen)
                alive.append(gen)
            except StopIteration:
                pass
        gens = alive


def _gdn_core_kernel(alog_ref, dtb_ref, q_ref, k_ref, v_ref, gate_ref, ab_ref, cw_ref, ng_ref,
                     o_ref, xc_ref, p_ref, ou_ref, m_ref, r_ref, egl_ref, tril_ref):
    hps = GDN_HEADS_PER_STEP
    hd = GDN_HEAD_DIM
    head0 = pl.program_id(1) * hps
    seq = q_ref.shape[1]
    cl = GDN_CHUNK
    n_chunks = seq // cl

    row = lax.broadcasted_iota(jnp.int32, (cl, cl), 0)
    col = lax.broadcasted_iota(jnp.int32, (cl, cl), 1)
    tril = row >= col
    strict = row > col
    tril_ref[...] = tril.astype(F32).astype(BF16)
    eye = (row == col).astype(F32)
    lane = lax.broadcasted_iota(jnp.int32, (cl, 128), 1)
    def same_block(size):
        shift = size.bit_length() - 1
        return jnp.right_shift(row, shift) == jnp.right_shift(col, shift)

    leaf_mask = same_block(GDN_INV_LEAF)
    merge_masks = {}
    size = GDN_INV_LEAF
    while size < cl:
        merge_masks[size] = same_block(2 * size) & jnp.logical_not(same_block(size))
        size *= 2

    conv_defs, prep_defs = [], []
    for hh in range(hps):
        head = head0 + hh
        lanes = slice(hh * hd, (hh + 1) * hd)

        def conv_slab(which, r0, delay, hh=hh, lanes=lanes):
            src_ref = (q_ref, k_ref, v_ref)[which]
            body = src_ref[0, pl.ds(r0, CONV_SLAB), lanes].astype(F32)
            h0 = pl.multiple_of(jnp.maximum(r0 - CONV_HALO, 0), CONV_HALO)
            halo = src_ref[0, pl.ds(h0, CONV_HALO), lanes].astype(F32)
            halo = jnp.where(r0 > 0, halo, 0.0)
            ext = jnp.concatenate([halo, body], axis=0)
            w = cw_ref[hh, which]
            for _ in range(1 + delay):
                yield
            acc = ext[CONV_HALO:] * w[GDN_CONV - 1:GDN_CONV]
            for sh in range(1, GDN_CONV):
                tap = GDN_CONV - 1 - sh
                acc = acc + pltpu.roll(ext, sh, 0)[CONV_HALO:] * w[tap:tap + 1]
            y = _silu(acc)
            if which < 2:
                y = y * lax.rsqrt(jnp.sum(y * y, axis=-1, keepdims=True) + EPS)
            if which == 0:
                y = y * (hd ** -0.5)
            yield
            xc_ref[hh % 2, which, pl.ds(r0, CONV_SLAB), :] = y

        conv_defs.append(conv_slab)

        a_head = jnp.exp(jnp.full((1, 1), alog_ref[head], F32))
        dt_head = dtb_ref[head]

        def prep_chunk(c, hh=hh, head=head, a_head=a_head, dt_head=dt_head):
            r0 = pl.multiple_of(c * cl, cl)
            q = xc_ref[hh % 2, 0, pl.ds(r0, cl), :]
            k = xc_ref[hh % 2, 1, pl.ds(r0, cl), :]
            v = xc_ref[hh % 2, 2, pl.ds(r0, cl), :]
            ab = ab_ref[0, pl.ds(r0, cl), :]
            yield
            a_col = jnp.sum(jnp.where(lane == head, ab, 0.0), axis=1, keepdims=True)
            b_col = jnp.sum(jnp.where(lane == head + GDN_HEADS, ab, 0.0), axis=1, keepdims=True)
            g = -a_head * _softplus(a_col + dt_head)
            beta = _sigmoid(b_col)
            g_hi, g_lo = _split(jnp.broadcast_to(g, (cl, 128)))
            tril_b = tril_ref[...]
            gc = _dot(tril_b, g_hi) + _dot(tril_b, g_lo)
            kb = k * beta
            k_b16 = k.astype(BF16)
            kk = _dot_nt(kb.astype(BF16), k_b16)
            qk = _dot_nt(q.astype(BF16), k_b16)
            yield
            gl = gc[cl - 1:cl, :]
            e_gc = jnp.exp(gc)
            e_rev = jnp.exp(gl - gc)
            e_gl = jnp.exp(gl)
            diff = gc[:, :cl] - gc[:, :cl].T
            dec = jnp.exp(jnp.where(tril, diff, -jnp.inf))
            lmat = jnp.where(strict, kk * dec, 0.0).astype(BF16).astype(F32)
            attn = (qk * dec).astype(BF16)
            rhs = jnp.concatenate([kb * e_gc, v * beta], axis=1).astype(BF16)
            kdt = (k * e_rev).T.astype(BF16)
            qd = q * e_gc
            pw = -jnp.where(leaf_mask, lmat, 0.0)
            tinv = eye + pw
            span = 2
            while span < GDN_INV_LEAF:
                pw_b = pw.astype(BF16)
                pw = _dot(pw_b, pw_b)
                yield
                tinv = tinv + _dot(tinv.astype(BF16), pw.astype(BF16))
                span *= 2
            size = GDN_INV_LEAF
            while size < cl:
                yield
                t_b = tinv.astype(BF16)
                xm = _dot(jnp.where(merge_masks[size], lmat, 0.0).astype(BF16), t_b)
                yield
                tinv = tinv - _dot(t_b, xm.astype(BF16))
                size *= 2
            yield
            wu = _dot(tinv.astype(BF16), rhs).astype(BF16)
            yield
            aw = _dot(attn, wu)
            kw = _dot(kdt, wu)
            yield
            p_ref[hh, c] = (qd - aw[:, :hd]).astype(BF16)
            ou_ref[hh, c] = aw[:, hd:]
            m_ref[hh, c] = (-kw[:, :hd]).astype(BF16)
            r_ref[hh, c] = kw[:, hd:]
            egl_ref[hh, c] = jnp.broadcast_to(e_gl, (8, 128))

        prep_defs.append(prep_chunk)

    groups = n_chunks // GDN_GROUP
    slabs_per_group = GDN_GROUP * cl // CONV_SLAB

    def conv_tasks(hh, i):
        tasks = []
        for t in range(slabs_per_group):
            r0 = pl.multiple_of((i * slabs_per_group + t) * CONV_SLAB, CONV_SLAB)
            for which in range(3):
                tasks.append(conv_defs[hh](which, r0, len(tasks)))
        return tasks

    def conv_only(i, carry):
        _run_interleaved(conv_tasks(0, i))
        return carry

    lax.fori_loop(0, groups, conv_only, 0)

    for hh in range(hps):
        def stage_body(i, carry, hh=hh):
            tasks = [prep_defs[hh](i * GDN_GROUP + u) for u in range(GDN_GROUP)]
            if hh + 1 < hps:
                tasks += conv_tasks(hh + 1, i)
            _run_interleaved(tasks)
            return carry

        lax.fori_loop(0, groups, stage_body, 0)

    ng = ng_ref[...]

    def scan_body(c, states):
        r0 = pl.multiple_of(c * cl, cl)
        new_states = [None] * hps

        def head_step(hh):
            s = states[hh]
            p = p_ref[hh, c]
            m = m_ref[hh, c]
            ou = ou_ref[hh, c]
            r = r_ref[hh, c]
            eg = egl_ref[hh, c][0:1, :]
            gt = gate_ref[0, pl.ds(r0, cl), hh * hd:(hh + 1) * hd].astype(F32)
            yield
            sb = s.astype(BF16)
            o = _dot(p, sb) + ou
            new_states[hh] = eg * s + (_dot(m, sb) + r)
            rn = lax.rsqrt(jnp.mean(o * o, axis=-1, keepdims=True) + EPS)
            y = ((o * rn) * ng * _silu(gt)).astype(o_ref.dtype)
            yield
            o_ref[0, pl.ds(r0, cl), hh * hd:(hh + 1) * hd] = y

        _run_interleaved([head_step(hh) for hh in range(hps)])
        return tuple(new_states)

    init = tuple(jnp.zeros((hd, hd), F32) for _ in range(hps))
    lax.fori_loop(0, n_chunks, scan_body, init)


def _gdn_core(main, ab, cw, a_log, dt_bias, norm_g, bsz, seq):
    hd = GDN_HEAD_DIM
    nh = GDN_HEADS
    hps = GDN_HEADS_PER_STEP
    groups = nh // hps
    n_chunks = seq // GDN_CHUNK
    main3 = main.reshape(bsz, seq, main.shape[1])
    ab3 = ab.reshape(bsz, seq, 128)
    smem = pl.BlockSpec(memory_space=pltpu.SMEM)

    def col_spec(which):
        return pl.BlockSpec((1, seq, hps * hd), lambda b, h, which=which: (b, 0, which * groups + h))

    return pl.pallas_call(
        _gdn_core_kernel,
        grid=(bsz, groups),
        in_specs=[
            smem, smem,
            col_spec(0), col_spec(1), col_spec(2), col_spec(3),
            pl.BlockSpec((1, seq, 128), lambda b, h: (b, 0, 0)),
            pl.BlockSpec((hps, 3, GDN_CONV, hd), lambda b, h: (h, 0, 0, 0)),
            pl.BlockSpec((1, hd), lambda b, h: (0, 0)),
        ],
        out_specs=pl.BlockSpec((1, seq, hps * hd), lambda b, h: (b, 0, h)),
        out_shape=jax.ShapeDtypeStruct((bsz, seq, nh * hd), BF16),
        scratch_shapes=[
            pltpu.VMEM((2, 3, seq, hd), F32),
            pltpu.VMEM((hps, n_chunks, GDN_CHUNK, hd), BF16),
            pltpu.VMEM((hps, n_chunks, GDN_CHUNK, hd), F32),
            pltpu.VMEM((hps, n_chunks, hd, hd), BF16),
            pltpu.VMEM((hps, n_chunks, hd, hd), F32),
            pltpu.VMEM((hps, n_chunks, 8, 128), F32),
            pltpu.VMEM((GDN_CHUNK, GDN_CHUNK), BF16),
        ],
        compiler_params=_cparams(("parallel", "parallel")),
        name="gdn_core",
    )(a_log, dt_bias, main3, main3, main3, main3, ab3, cw, norm_g)


def _mla_proj_kernel(x_ref, g_ref, sc_ref, sh_ref, pos_ref, tab_ref, win_ref, qg_ref, kvg_ref,
                     wq_ref, wqs_ref, wkn_ref, wv_ref, q_ref, kn_ref, kr_ref, v_ref):
    h = _norm_mod(x_ref[...], g_ref[...], sc_ref[0], sh_ref[0]).astype(BF16)
    proj = _dot(h, win_ref[...])
    qr, kvr = MLA_Q_RANK, MLA_KV_RANK
    c_q = proj[:, :qr]
    c_kv = proj[:, qr:qr + kvr]
    kr_a = proj[:, qr + kvr:qr + kvr + 128]
    kr_b = proj[:, qr + kvr + 128:qr + kvr + 256]

    def rms(t, gain):
        return (t * lax.rsqrt(jnp.mean(t * t, axis=-1, keepdims=True) + EPS)) * gain

    cqn = rms(c_q, qg_ref[...])
    ckvn = rms(c_kv, kvg_ref[...])
    cqn_t = cqn.T.astype(BF16)
    ckvn_t = ckvn.T.astype(BF16)
    ckvn = ckvn.astype(BF16)

    ang = pos_ref[...].astype(F32) * tab_ref[0:1, :]
    ct = jnp.cos(ang) * tab_ref[1:2, :]
    st = jnp.sin(ang) * tab_ref[2:3, :]
    kr_ref[...] = (kr_a * ct + kr_b * st).astype(kr_ref.dtype)
    ct_t = ct.T[:MLA_ROPE, :]
    st_t = st.T[:MLA_ROPE, :]

    scale = (MLA_NOPE + MLA_ROPE) ** -0.5 * LOG2_E
    qf = _dot(wq_ref[...], cqn_t)
    qs = _dot(wqs_ref[...], cqn_t)
    for hh in range(MLA_HEADS):
        base = hh * MLA_QK_PAD
        rope0 = base + MLA_NOPE
        q_ref[0, base:rope0, :] = (qf[base:rope0, :] * scale).astype(q_ref.dtype)
        rot = qf[rope0:rope0 + MLA_ROPE, :] * ct_t + qs[hh * MLA_ROPE:(hh + 1) * MLA_ROPE, :] * st_t
        q_ref[0, rope0:rope0 + MLA_ROPE, :] = (rot * scale).astype(q_ref.dtype)
        q_ref[0, rope0 + MLA_ROPE:base + MLA_QK_PAD, :] = jnp.zeros(
            (MLA_QK_PAD - MLA_NOPE - MLA_ROPE, q_ref.shape[2]), q_ref.dtype)

    kn_ref[...] = _dot(ckvn, wkn_ref[...]).astype(kn_ref.dtype)
    v_ref[0] = _dot(wv_ref[...], ckvn_t).astype(v_ref.dtype)


def _mla_proj(x2, g, mod, pos2, tab, w_in, qg, kvg, wq, wqs, wkn, wv, seq):
    n, d = x2.shape
    tm = ROW_TILE
    per_b = seq // tm
    full = lambda a: pl.BlockSpec(a.shape, lambda i: (0,) * a.ndim)
    hq = MLA_HEADS * MLA_QK_PAD
    hv = MLA_HEADS * MLA_V
    return pl.pallas_call(
        _mla_proj_kernel,
        grid=(n // tm,),
        in_specs=[
            pl.BlockSpec((tm, d), lambda i: (i, 0)),
            full(g),
            pl.BlockSpec((1, 1, d), lambda i: ((i // per_b) * N_MOD + 1, 0, 0)),
            pl.BlockSpec((1, 1, d), lambda i: ((i // per_b) * N_MOD + 0, 0, 0)),
            pl.BlockSpec((tm, 1), lambda i: (i, 0)),
            full(tab), full(w_in), full(qg), full(kvg), full(wq), full(wqs), full(wkn), full(wv),
        ],
        out_specs=[
            pl.BlockSpec((1, hq, tm), lambda i: (i // per_b, 0, i % per_b)),
            pl.BlockSpec((tm, hv), lambda i: (i, 0)),
            pl.BlockSpec((tm, 128), lambda i: (i, 0)),
            pl.BlockSpec((1, hv, tm), lambda i: (i // per_b, 0, i % per_b)),
        ],
        out_shape=[
            jax.ShapeDtypeStruct((n // seq, hq, seq), BF16),
            jax.ShapeDtypeStruct((n, hv), BF16),
            jax.ShapeDtypeStruct((n, 128), BF16),
            jax.ShapeDtypeStruct((n // seq, hv, seq), BF16),
        ],
        compiler_params=_cparams(("parallel",)),
        name="mla_proj",
    )(x2, g, mod, mod, pos2, tab, w_in, qg, kvg, wq, wqs, wkn, wv)


def _flash_kernel(q_ref, kn_ref, kr_ref, v_ref, o_ref):
    blk = ATTN_BLOCK
    n_blocks = kn_ref.shape[1] // blk

    def attend(qi):
        n_keys = qi + 1
        q_t = q_ref[0, :, qi * blk:(qi + 1) * blk]
        m = jnp.full((1, blk), -jnp.inf, F32)
        l = jnp.zeros((1, blk), F32)
        acc = jnp.zeros((MLA_V, blk), F32)
        kb = ATTN_KEY_BLOCK
        n_sub = n_keys * (blk // kb)
        diag0 = (n_keys - 1) * blk

        def logits(j):
            ks = slice(j * kb, (j + 1) * kb)
            kcat = jnp.concatenate([kn_ref[0, ks, :], kr_ref[0, ks, :]], axis=1)
            return _dot(kcat, q_t)

        s_next = logits(0)
        for j in range(n_sub):
            s = s_next
            if j + 1 < n_sub:
                s_next = logits(j + 1)
            if (j + 1) * kb > diag0:
                key = lax.broadcasted_iota(jnp.int32, s.shape, 0) + (j * kb - diag0)
                qry = lax.broadcasted_iota(jnp.int32, s.shape, 1)
                s = jnp.where(key <= qry, s, -jnp.inf)
            m_new = jnp.maximum(m, jnp.max(s, axis=0, keepdims=True))
            alpha = jnp.exp2(m - m_new)
            p = jnp.exp2(s - m_new)
            l = alpha * l + jnp.sum(p, axis=0, keepdims=True)
            acc = alpha * acc + _dot(v_ref[0, :, j * kb:(j + 1) * kb], p.astype(BF16))
            m = m_new
        o_ref[0, qi * blk:(qi + 1) * blk, :] = (acc / l).T.astype(o_ref.dtype)

    for i in range(n_blocks):
        attend(i)


def _flash_attention(q_t, kn, kr, v_t, bsz, seq):
    nh = MLA_HEADS
    blk = ATTN_BLOCK
    kn3 = kn.reshape(bsz, seq, nh * MLA_NOPE)
    kr3 = kr.reshape(bsz, seq, 128)
    return pl.pallas_call(
        _flash_kernel,
        grid=(bsz, nh),
        in_specs=[
            pl.BlockSpec((1, MLA_QK_PAD, seq), lambda b, h: (b, h, 0)),
            pl.BlockSpec((1, seq, MLA_NOPE), lambda b, h: (b, 0, h)),
            pl.BlockSpec((1, seq, 128), lambda b, h: (b, 0, 0)),
            pl.BlockSpec((1, MLA_V, seq), lambda b, h: (b, h, 0)),
        ],
        out_specs=pl.BlockSpec((1, seq, MLA_V), lambda b, h: (b, 0, h)),
        out_shape=jax.ShapeDtypeStruct((bsz, seq, nh * MLA_V), BF16),
        compiler_params=_cparams(("parallel", "parallel")),
        name="mla_flash",
    )(q_t, kn3, kr3, v_t)


def _ffn_kernel(a_ref, wo_ref, x_ref, gm_ref, g_ref, sc_ref, sh_ref, gt_ref, wg_ref, wu_ref, wd_ref,
                fg_ref, o_ref, *, final):
    xm = x_ref[...] + gm_ref[0] * _dot(a_ref[...], wo_ref[...])
    h = _norm_mod(xm, g_ref[...], sc_ref[0], sh_ref[0]).astype(BF16)
    dff = wg_ref.shape[1]
    acc = jnp.zeros(xm.shape, F32)
    for f0 in range(0, dff, dff // FFN_SPLIT):
        cols = slice(f0, f0 + dff // FFN_SPLIT)
        act = (_silu(_dot(h, wg_ref[:, cols])) * _dot(h, wu_ref[:, cols])).astype(BF16)
        acc = acc + _dot(act, wd_ref[cols, :])
    xn = xm + gt_ref[0] * acc
    if final:
        xn = (xn * lax.rsqrt(jnp.mean(xn * xn, axis=-1, keepdims=True) + EPS)) * fg_ref[...]
    o_ref[...] = xn


def _mixer_out_ffn(a2, w_out, x2, g, mod, wg, wu, wd, layer, fg, seq, final):
    n, d = x2.shape
    tm = ROW_TILE
    per_b = seq // tm
    mod_spec = lambda k: pl.BlockSpec((1, 1, d), lambda i, k=k: ((i // per_b) * N_MOD + k, 0, 0))
    resident = lambda a: pl.BlockSpec(a.shape, lambda i: (0,) * a.ndim, pipeline_mode=pl.Buffered(1))
    layer_resident = lambda a: pl.BlockSpec((None,) + a.shape[1:], lambda i: (layer, 0, 0),
                                            pipeline_mode=pl.Buffered(1))
    return pl.pallas_call(
        functools.partial(_ffn_kernel, final=final),
        grid=(n // tm,),
        in_specs=[
            pl.BlockSpec((tm, a2.shape[1]), lambda i: (i, 0)),
            resident(w_out),
            pl.BlockSpec((tm, d), lambda i: (i, 0)),
            mod_spec(2),
            pl.BlockSpec((1, d), lambda i: (0, 0)),
            mod_spec(4), mod_spec(3), mod_spec(5),
            layer_resident(wg), layer_resident(wu), layer_resident(wd),
            pl.BlockSpec((1, d), lambda i: (0, 0)),
        ],
        out_specs=pl.BlockSpec((tm, d), lambda i: (i, 0)),
        out_shape=jax.ShapeDtypeStruct((n, d), F32),
        compiler_params=_cparams(("parallel",)),
        name="ffn_final" if final else "ffn",
    )(a2, w_out, x2, mod, g, mod, mod, mod, wg, wu, wd, fg)


def _gdn_weights(w_in, conv_w):
    main_cols = 2 * GDN_HEADS * GDN_HEAD_DIM + 2 * GDN_HEADS * GDN_HEAD_DIM
    w_main = w_in[:, :main_cols].astype(BF16)
    w_ab = jnp.pad(w_in[:, main_cols:], ((0, 0), (0, 128 - 2 * GDN_HEADS))).astype(BF16)
    cw = conv_w.T.reshape(GDN_CONV, 3, GDN_HEADS, GDN_HEAD_DIM).transpose(2, 1, 0, 3)
    return w_main, w_ab, cw


def _mla_weights(w_in, w_uq, w_ukv):
    qr, kvr, half = MLA_Q_RANK, MLA_KV_RANK, MLA_ROPE // 2
    d = w_in.shape[0]
    rope = w_in[:, qr + kvr:]
    z = jnp.zeros((d, 128 - MLA_ROPE), w_in.dtype)
    w_in_ext = jnp.concatenate(
        [w_in[:, :qr + kvr], rope, z, rope[:, half:], rope[:, :half], z], axis=1).astype(BF16)
    uq = w_uq.reshape(qr, MLA_HEADS, MLA_NOPE + MLA_ROPE)
    nope, r = uq[..., :MLA_NOPE], uq[..., MLA_NOPE:]
    zq = jnp.zeros((qr, MLA_HEADS, 128 - MLA_ROPE), w_uq.dtype)
    wq = jnp.concatenate([nope, r, zq], axis=-1).reshape(qr, MLA_HEADS * MLA_QK_PAD).T.astype(BF16)
    wqs = jnp.concatenate([r[..., half:], r[..., :half]], axis=-1).reshape(
        qr, MLA_HEADS * MLA_ROPE).T.astype(BF16)
    ukv = w_ukv.reshape(kvr, MLA_HEADS, MLA_NOPE + MLA_V)
    wkn = ukv[..., :MLA_NOPE].reshape(kvr, MLA_HEADS * MLA_NOPE).astype(BF16)
    wv = ukv[..., MLA_NOPE:].reshape(kvr, MLA_HEADS * MLA_V).T.astype(BF16)
    return w_in_ext, wq, wqs, wkn, wv


def _rope_table():
    half = MLA_ROPE // 2
    inv_freq = ROPE_THETA ** (-jnp.arange(0, MLA_ROPE, 2, dtype=F32) / MLA_ROPE)
    z = jnp.zeros((128 - MLA_ROPE,), F32)
    ones = jnp.ones((half,), F32)
    freq = jnp.concatenate([inv_freq, inv_freq, z])
    cmask = jnp.concatenate([ones, ones, z])
    ssign = jnp.concatenate([-ones, ones, z])
    pad = jnp.zeros((5, 128), F32)
    return jnp.concatenate([jnp.stack([freq, cmask, ssign]), pad], axis=0)


def kernel(x, c, positions, ada_w, ada_b, norm_mix_g, norm_ffn_g, gdn_w_in, gdn_conv_w, gdn_a_log,
           gdn_dt_bias, gdn_norm_g, gdn_w_out, mla_w_in, mla_q_norm_g, mla_kv_norm_g, mla_w_uq,
           mla_w_ukv, mla_w_out, ffn_w_gate, ffn_w_up, ffn_w_down, final_norm_g):
    bsz, seq, d = x.shape
    depth = ada_w.shape[0]
    n_mixers = 2
    x2 = x.reshape(bsz * seq, d)
    pos2 = positions.reshape(bsz * seq, 1)
    tab = _rope_table()

    mod_all = _adaln(c, ada_w, ada_b).reshape(depth, bsz * N_MOD, 1, d)
    ffn_wg, ffn_wu, ffn_wd = (w.astype(BF16) for w in (ffn_w_gate, ffn_w_up, ffn_w_down))

    for layer in range(depth):
        mod = mod_all[layer]
        j = layer // n_mixers
        g_mix = norm_mix_g[layer].reshape(1, d)
        if layer % n_mixers == 0:
            w_main, w_ab, cw = _gdn_weights(gdn_w_in[j], gdn_conv_w[j])
            main, ab = _gdn_inproj(x2, g_mix, mod, w_main, w_ab, seq)
            o = _gdn_core(main, ab, cw, gdn_a_log[j], gdn_dt_bias[j],
                          gdn_norm_g[j].reshape(1, GDN_HEAD_DIM), bsz, seq)
            w_out = gdn_w_out[j].astype(BF16)
        else:
            w_in_ext, wq, wqs, wkn, wv = _mla_weights(mla_w_in[j], mla_w_uq[j], mla_w_ukv[j])
            qcat, kn, kr, v = _mla_proj(
                x2, g_mix, mod, pos2, tab, w_in_ext,
                mla_q_norm_g[j].reshape(1, -1), mla_kv_norm_g[j].reshape(1, -1),
                wq, wqs, wkn, wv, seq)
            o = _flash_attention(qcat, kn, kr, v, bsz, seq)
            w_out = mla_w_out[j].astype(BF16)

        x2 = _mixer_out_ffn(o.reshape(bsz * seq, -1), w_out, x2, norm_ffn_g[layer].reshape(1, d), mod,
                            ffn_wg, ffn_wu, ffn_wd, layer, final_norm_g.reshape(1, d), seq,
                            final=(layer == depth - 1))

    return x2.reshape(bsz, seq, d)
```
